```python
import math
import jax, jax.numpy as jnp
from jax import lax
import numpy as np

D_MODEL = 4096
BATCH = 2
SEQ = 8192
DEPTH = 2

MIX_WIDTH = D_MODEL
HEAD_DIM = 128
HGRN_WIDTH = MIX_WIDTH // 2
HGRN_HEADS = HGRN_WIDTH // HEAD_DIM
HGRN_DK = HEAD_DIM
HGRN_DV = HEAD_DIM
HGRN_CHUNK = 64
ATTN_WIDTH = MIX_WIDTH - HGRN_WIDTH
ATTN_Q_HEADS = ATTN_WIDTH // HEAD_DIM
ATTN_KV_HEADS = ATTN_Q_HEADS // 4
ATTN_GROUP = ATTN_Q_HEADS // ATTN_KV_HEADS
ATTN_KV_WIDTH = ATTN_KV_HEADS * HEAD_DIM
WINDOW = 128
ATTN_BLOCK = 128
ROPE_THETA = 500000.0
ROPE_DIM = HEAD_DIM // 4
AB_IN_SIZES = (HGRN_WIDTH, HGRN_WIDTH, HGRN_WIDTH, HGRN_WIDTH, HGRN_WIDTH, ATTN_WIDTH, ATTN_KV_WIDTH, ATTN_KV_WIDTH)
AB_IN_COLS = 5 * HGRN_WIDTH + ATTN_WIDTH + 2 * ATTN_KV_WIDTH
GMLP_WIDTH = MIX_WIDTH
GMLP_CHUNK = 128
GMLP_GROUP_DIM = 128
GMLP_GROUPS = GMLP_WIDTH // GMLP_GROUP_DIM
MEM_TOKENS = 256
XA_HEADS = 4
XA_HEAD_DIM = D_MODEL // XA_HEADS
FFN_HIDDEN = -(-8 * D_MODEL // (3 * 256)) * 256
N_EVEN = (DEPTH + 1) // 2
N_ODD = DEPTH // 2
DEEPNORM_ALPHA = (2.0 * DEPTH) ** 0.25
DEEPNORM_BETA = (8.0 * DEPTH) ** -0.25
LN_EPS = 1e-5
RMS_EPS = 1e-6

kernel_name = "hybrid_hgrn2_swa_gmlp_deepnorm_encoder"


def layer_norm(x, g, b):
    xf = x.astype(jnp.float32)
    mu = jnp.mean(xf, axis=-1, keepdims=True)
    var = jnp.mean(jnp.square(xf - mu), axis=-1, keepdims=True)
    y = (xf - mu) * lax.rsqrt(var + LN_EPS) * g.astype(jnp.float32) + b.astype(jnp.float32)
    return y.astype(x.dtype)


def partial_rotary(x, positions):
    half = ROPE_DIM // 2
    inv_freq = jnp.power(jnp.float32(ROPE_THETA), -jnp.arange(half, dtype=jnp.float32) * (2.0 / ROPE_DIM))
    ang = positions.astype(jnp.float32)[..., None] * inv_freq
    cos = jnp.cos(ang)[:, :, None, :]
    sin = jnp.sin(ang)[:, :, None, :]
    xf = x.astype(jnp.float32)
    x1 = xf[..., :half]
    x2 = xf[..., half:ROPE_DIM]
    out = jnp.concatenate([x1 * cos - x2 * sin, x2 * cos + x1 * sin, xf[..., ROPE_DIM:]], axis=-1)
    return out.astype(x.dtype)


def hgrn2_bidirectional(q, f_logit, i, lb):
    B, N, H, Dk = q.shape
    Dv = i.shape[-1]
    C = HGRN_CHUNK
    nc = N // C
    lb = lb.astype(jnp.float32).reshape(2, 1, 1, H, Dk)
    z = f_logit.astype(jnp.float32)
    log_f = jnp.logaddexp(jnp.log(lb), jnp.log1p(-lb) + jax.nn.log_sigmoid(z))
    k = (1.0 - lb) * jax.nn.sigmoid(-z)
    qs = jax.nn.silu(q.astype(jnp.float32))
    vs = i.astype(jnp.float32)
    qd = jnp.stack([qs, qs[:, ::-1]])
    vd = jnp.stack([vs, vs[:, ::-1]])
    gd = jnp.stack([log_f[0], log_f[1][:, ::-1]])
    kd = jnp.stack([k[0], k[1][:, ::-1]])

    def to_chunks(a):
        return a.reshape(2, B, nc, C, H, a.shape[-1]).transpose(2, 0, 1, 4, 3, 5)

    causal_in_scan = jnp.tril(jnp.ones((C, C), dtype=bool))[:, :, None]

    def step(S, inp):
        qc, kc, vc, gc = inp
        b = jnp.cumsum(gc, axis=-2)
        inter = jnp.einsum('zbhtk,zbhkv->zbhtv', qc * jnp.exp(b), S)
        diff = b[..., :, None, :] - b[..., None, :, :]
        decay = jnp.exp(jnp.where(causal_in_scan, diff, -jnp.inf))
        scores = jnp.einsum('zbhtk,zbhtsk,zbhsk->zbhts', qc, decay, kc)
        intra = jnp.einsum('zbhts,zbhsv->zbhtv', scores, vc)
        b_last = b[..., -1:, :]
        S_new = jnp.exp(b_last[..., 0, :])[..., None] * S + jnp.einsum(
            'zbhsk,zbhsv->zbhkv', kc * jnp.exp(b_last - b), vc)
        return S_new, inter + intra

    S0 = jnp.zeros((2, B, H, Dk, Dv), jnp.float32)
    _, ys = lax.scan(step, S0, (to_chunks(qd), to_chunks(kd), to_chunks(vd), to_chunks(gd)))
    o = ys.transpose(1, 2, 0, 4, 3, 5).reshape(2, B, N, H, Dv)
    return o[0] + o[1][:, ::-1]


def window_attention_with_sink(q, k, v, sink):
    B, N, Hq, hd = q.shape
    L = ATTN_BLOCK
    nb = N // L
    qb = q.reshape(B, nb, L, ATTN_KV_HEADS, ATTN_GROUP, hd)

    def band(a):
        ap = jnp.pad(a, ((0, 0), (L, L), (0, 0), (0, 0))).reshape(B, nb + 2, L, ATTN_KV_HEADS, hd)
        return jnp.concatenate([ap[:, :-2], ap[:, 1:-1], ap[:, 2:]], axis=2)

    kb = band(k)
    vb = band(v)
    s = jnp.einsum('bnqhgd,bnkhd->bhgnqk', qb, kb).astype(jnp.float32) * (hd ** -0.5)
    qpos = jnp.arange(nb)[:, None, None] * L + jnp.arange(L)[None, :, None]
    kpos = jnp.arange(nb)[:, None, None] * L - L + jnp.arange(3 * L)[None, None, :]
    valid = (jnp.abs(qpos - kpos) <= WINDOW) & (kpos >= 0) & (kpos < N)
    s = jnp.where(valid, s, -jnp.inf)
    sk = sink.astype(jnp.float32).reshape(ATTN_KV_HEADS, ATTN_GROUP)[None, :, :, None, None, None]
    m = jnp.maximum(jnp.max(s, axis=-1, keepdims=True), sk)
    p = jnp.exp(s - m)
    probs = p / (jnp.sum(p, axis=-1, keepdims=True) + jnp.exp(sk - m))
    o = jnp.einsum('bhgnqk,bnkhd->bnqhgd', probs.astype(v.dtype), vb)
    return o.reshape(B, N, Hq * hd)


def hgrn2_swa_mixer(x, positions, w_in, lb, norm_g, sink, w_out):
    B, N, _ = x.shape
    proj = x @ w_in
    split_at = [int(c) for c in np.cumsum(AB_IN_SIZES)[:-1]]
    hq, hf_fwd, hf_bwd, hi, hg, aq, ak, av = jnp.split(proj, split_at, axis=-1)
    f_logit = jnp.stack([hf_fwd, hf_bwd]).reshape(2, B, N, HGRN_HEADS, HGRN_DK)
    o = hgrn2_bidirectional(hq.reshape(B, N, HGRN_HEADS, HGRN_DK), f_logit,
                            hi.reshape(B, N, HGRN_HEADS, HGRN_DV), lb)
    o = o * lax.rsqrt(jnp.mean(jnp.square(o), axis=-1, keepdims=True) + RMS_EPS)
    o = o * norm_g.astype(jnp.float32).reshape(HGRN_HEADS, HGRN_DV)
    a_out = (o.reshape(B, N, HGRN_WIDTH) * jax.nn.silu(hg.astype(jnp.float32))).astype(x.dtype)
    q = partial_rotary(aq.reshape(B, N, ATTN_Q_HEADS, HEAD_DIM), positions)
    k = partial_rotary(ak.reshape(B, N, ATTN_KV_HEADS, HEAD_DIM), positions)
    v = av.reshape(B, N, ATTN_KV_HEADS, HEAD_DIM)
    b_out = window_attention_with_sink(q, k, v, sink)
    return jnp.concatenate([a_out, b_out], axis=-1) @ w_out


def chunked_spatial_gating(x, w_in, norm_g, norm_b, w_s, b_s, w_out):
    B, N, _ = x.shape
    z = jax.nn.gelu(x @ w_in, approximate=False)
    u, v = jnp.split(z, 2, axis=-1)
    v = layer_norm(v, norm_g, norm_b)
    nc = N // GMLP_CHUNK
    vg = v.reshape(B, nc, GMLP_CHUNK, GMLP_GROUPS, GMLP_GROUP_DIM)
    s = jnp.einsum('gts,bcsge->bctge', w_s, vg) + b_s.T[None, None, :, :, None]
    return (u * s.reshape(B, N, GMLP_WIDTH)) @ w_out


def memory_cross_attention(x, mem, w_q, w_k, w_v, w_o):
    B, N, _ = x.shape
    M = mem.shape[1]
    q = (x @ w_q).reshape(B, N, XA_HEADS, XA_HEAD_DIM)
    k = (mem @ w_k).reshape(B, M, XA_HEADS, XA_HEAD_DIM)
    v = (mem @ w_v).reshape(B, M, XA_HEADS, XA_HEAD_DIM)
    s = jnp.einsum('bnhd,bmhd->bhnm', q, k).astype(jnp.float32) * (XA_HEAD_DIM ** -0.5)
    p = jax.nn.softmax(s, axis=-1).astype(v.dtype)
    o = jnp.einsum('bhnm,bmhd->bnhd', p, v).reshape(B, N, XA_HEADS * XA_HEAD_DIM)
    return o @ w_o


def swiglu_ffn(x, w_gate, w_up, w_down):
    return (jax.nn.silu(x @ w_gate) * (x @ w_up)) @ w_down


def setup_inputs(seed: int = 0) -> dict:
    key = jax.random.key(seed)
    ks = jax.random.split(key, 24)
    f32 = jnp.float32

    def dense(k, shape, fan_in, scale=1.0):
        return jax.random.normal(k, shape, f32) * (scale * fan_in ** -0.5)

    x = jax.random.normal(ks[0], (BATCH, SEQ, D_MODEL), f32)
    mem = jax.random.normal(ks[1], (BATCH, MEM_TOKENS, D_MODEL), f32)
    offsets = jax.random.randint(ks[2], (BATCH, 1), 0, 4096, dtype=jnp.int32)
    positions = jnp.arange(SEQ, dtype=jnp.int32)[None, :] + offsets
    i_start = 3 * HGRN_WIDTH
    col_scale = jnp.ones((AB_IN_COLS,), f32)
    col_scale = col_scale.at[i_start:i_start + HGRN_WIDTH].set(DEEPNORM_BETA)
    col_scale = col_scale.at[AB_IN_COLS - ATTN_KV_WIDTH:].set(DEEPNORM_BETA)
    ab_w_in = dense(ks[3], (N_EVEN, D_MODEL, AB_IN_COLS), D_MODEL) * col_scale
    hgrn_lower_bounds = 0.5 * jax.random.normal(ks[4], (2, DEPTH + 1, HGRN_WIDTH), f32)
    hgrn_norm_g = 1.0 + 0.02 * jax.random.normal(ks[5], (N_EVEN, HGRN_WIDTH), f32)
    attn_sink = 0.5 * jax.random.normal(ks[6], (N_EVEN, ATTN_Q_HEADS), f32)
    ab_w_out = dense(ks[7], (N_EVEN, MIX_WIDTH, D_MODEL), MIX_WIDTH, DEEPNORM_BETA)
    gmlp_w_in = dense(ks[8], (N_ODD, D_MODEL, 2 * GMLP_WIDTH), D_MODEL)
    gmlp_norm_g = 1.0 + 0.02 * jax.random.normal(ks[9], (N_ODD, GMLP_WIDTH), f32)
    gmlp_norm_b = 0.02 * jax.random.normal(ks[10], (N_ODD, GMLP_WIDTH), f32)
    gmlp_w_spatial = dense(ks[11], (N_ODD, GMLP_GROUPS, GMLP_CHUNK, GMLP_CHUNK), GMLP_CHUNK)
    gmlp_b_spatial = 1.0 + 0.02 * jax.random.normal(ks[12], (N_ODD, GMLP_GROUPS, GMLP_CHUNK), f32)
    gmlp_w_out = dense(ks[13], (N_ODD, GMLP_WIDTH, D_MODEL), GMLP_WIDTH, DEEPNORM_BETA)
    xa_w_q = dense(ks[14], (DEPTH, D_MODEL, D_MODEL), D_MODEL)
    xa_w_k = dense(ks[15], (DEPTH, D_MODEL, D_MODEL), D_MODEL)
    xa_w_v = dense(ks[16], (DEPTH, D_MODEL, D_MODEL), D_MODEL, DEEPNORM_BETA)
    xa_w_o = dense(ks[17], (DEPTH, D_MODEL, D_MODEL), D_MODEL, DEEPNORM_BETA)
    ffn_w_gate = dense(ks[18], (DEPTH, D_MODEL, FFN_HIDDEN), D_MODEL)
    ffn_w_up = dense(ks[19], (DEPTH, D_MODEL, FFN_HIDDEN), D_MODEL)
    ffn_w_down = dense(ks[20], (DEPTH, FFN_HIDDEN, D_MODEL), FFN_HIDDEN, DEEPNORM_BETA)
    ln_g = 1.0 + 0.02 * jax.random.normal(ks[21], (DEPTH, 3, D_MODEL), f32)
    ln_b = 0.02 * jax.random.normal(ks[22], (DEPTH, 3, D_MODEL), f32)
    return {"x": x, "mem": mem, "positions": positions,
            "ab_w_in": ab_w_in, "hgrn_lower_bounds": hgrn_lower_bounds, "hgrn_norm_g": hgrn_norm_g,
            "attn_sink": attn_sink, "ab_w_out": ab_w_out,
            "gmlp_w_in": gmlp_w_in, "gmlp_norm_g": gmlp_norm_g, "gmlp_norm_b": gmlp_norm_b,
            "gmlp_w_spatial": gmlp_w_spatial, "gmlp_b_spatial": gmlp_b_spatial, "gmlp_w_out": gmlp_w_out,
            "xa_w_q": xa_w_q, "xa_w_k": xa_w_k, "xa_w_v": xa_w_v, "xa_w_o": xa_w_o,
            "ffn_w_gate": ffn_w_gate, "ffn_w_up": ffn_w_up, "ffn_w_down": ffn_w_down,
            "ln_g": ln_g, "ln_b": ln_b}


def reference(x, mem, positions, ab_w_in, hgrn_lower_bounds, hgrn_norm_g, attn_sink, ab_w_out,
              gmlp_w_in, gmlp_norm_g, gmlp_norm_b, gmlp_w_spatial, gmlp_b_spatial, gmlp_w_out,
              xa_w_q, xa_w_k, xa_w_v, xa_w_o, ffn_w_gate, ffn_w_up, ffn_w_down, ln_g, ln_b):
    lb_all = jnp.cumsum(jax.nn.softmax(hgrn_lower_bounds.astype(jnp.float32), axis=1), axis=1)
    h = x
    for layer in range(DEPTH):
        j = layer // 2
        if layer % 2 == 0:
            mix = hgrn2_swa_mixer(h, positions, ab_w_in[j], lb_all[:, layer], hgrn_norm_g[j],
                                  attn_sink[j], ab_w_out[j])
        else:
            mix = chunked_spatial_gating(h, gmlp_w_in[j], gmlp_norm_g[j], gmlp_norm_b[j],
                                         gmlp_w_spatial[j], gmlp_b_spatial[j], gmlp_w_out[j])
        h = layer_norm(DEEPNORM_ALPHA * h + mix, ln_g[layer, 0], ln_b[layer, 0])
        xa = memory_cross_attention(h, mem, xa_w_q[layer], xa_w_k[layer], xa_w_v[layer], xa_w_o[layer])
        h = layer_norm(DEEPNORM_ALPHA * h + xa, ln_g[layer, 1], ln_b[layer, 1])
        ff = swiglu_ffn(h, ffn_w_gate[layer], ffn_w_up[layer], ffn_w_down[layer])
        h = layer_norm(DEEPNORM_ALPHA * h + ff, ln_g[layer, 2], ln_b[layer, 2])
    return h
```

```python
import functools
import math

import jax
import jax.numpy as jnp
from jax import lax
from jax.experimental import pallas as pl
from jax.experimental.pallas import tpu as pltpu

F32 = jnp.float32
BF16 = jnp.bfloat16

LANES = 128
HEAD_DIM = LANES
VMEM_LIMIT_BYTES = 58 * 1024 * 1024

HGRN_CHUNK = 128
HGRN_LEVELS = 7
WINDOW = 128
ATTN_BLOCK = 128
ATTN_GROUP = 4
ROPE_THETA = 500000.0
ROPE_DIM = HEAD_DIM // 4
GMLP_CHUNK = 128
MEM_HEADS = 4
LN_EPS = 1e-5
RMS_EPS = 1e-6


def _params(*sem):
    return pltpu.CompilerParams(dimension_semantics=sem, vmem_limit_bytes=VMEM_LIMIT_BYTES)


def _pick(n, pref):
    t = min(pref, n)
    while n % t:
        t -= LANES
    return t


def _sigmoid(x):
    return 1.0 / (1.0 + jnp.exp(-x))


def _identity(x):
    return x


def _gelu_exact(x):
    return 0.5 * x * (1.0 + lax.erf(x * (1.0 / math.sqrt(2.0))))


def _mm_body(*refs, n_pairs, epilogue):
    o_ref = refs[-1]
    acc = None
    for p in range(n_pairs):
        a = refs[2 * p][...]
        w = refs[2 * p + 1][...].astype(BF16)
        d = jnp.dot(a, w, preferred_element_type=F32)
        acc = d if acc is None else acc + d
    o_ref[...] = epilogue(acc).astype(o_ref.dtype)


def _matmul(pairs, out_dtype, *, tm, tn, epilogue=_identity, name):
    m = pairs[0][0].shape[0]
    n = pairs[0][1].shape[1]
    tm = _pick(m, tm)
    tn = _pick(n, tn)
    in_specs, args = [], []
    for a, w in pairs:
        k = a.shape[1]
        in_specs += [pl.BlockSpec((tm, k), lambda i, j: (i, 0)),
                     pl.BlockSpec((k, tn), lambda i, j: (0, j))]
        args += [a, w]
    return pl.pallas_call(
        functools.partial(_mm_body, n_pairs=len(pairs), epilogue=epilogue),
        out_shape=jax.ShapeDtypeStruct((m, n), out_dtype),
        grid=(m // tm, n // tn),
        in_specs=in_specs,
        out_specs=pl.BlockSpec((tm, tn), lambda i, j: (i, j)),
        compiler_params=_params("parallel", "arbitrary"),
        name=name,
    )(*args)


def _ffn_up_body(a_ref, wg_ref, wu_ref, o_ref):
    a = a_ref[...]
    g = jnp.dot(a, wg_ref[...], preferred_element_type=F32)
    u = jnp.dot(a, wu_ref[...], preferred_element_type=F32)
    o_ref[...] = (g * _sigmoid(g) * u).astype(o_ref.dtype)


def _ffn_up(a, wg, wu, *, tm, tn, name):
    m, k = a.shape
    n = wg.shape[1]
    tm = _pick(m, tm)
    tn = _pick(n, tn)
    return pl.pallas_call(
        _ffn_up_body,
        out_shape=jax.ShapeDtypeStruct((m, n), BF16),
        grid=(m // tm, n // tn),
        in_specs=[pl.BlockSpec((tm, k), lambda i, j: (i, 0)),
                  pl.BlockSpec((k, tn), lambda i, j: (0, j)),
                  pl.BlockSpec((k, tn), lambda i, j: (0, j))],
        out_specs=pl.BlockSpec((tm, tn), lambda i, j: (i, j)),
        compiler_params=_params("parallel", "arbitrary"),
        name=name,
    )(a, wg, wu)


def _ln_body(h_ref, m_ref, g_ref, b_ref, of_ref, ob_ref, *, alpha):
    y = alpha * h_ref[...] + m_ref[...]
    mu = jnp.mean(y, axis=-1, keepdims=True)
    yc = y - mu
    var = jnp.mean(yc * yc, axis=-1, keepdims=True)
    o = yc * lax.rsqrt(var + LN_EPS) * g_ref[...] + b_ref[...]
    of_ref[...] = o
    ob_ref[...] = o.astype(BF16)


def _residual_layer_norm(h, mix, g, b, *, alpha, tr, name):
    t, d = h.shape
    tr = _pick(t, tr)
    row = pl.BlockSpec((tr, d), lambda i: (i, 0))
    vec = pl.BlockSpec((1, d), lambda i: (0, 0))
    return pl.pallas_call(
        functools.partial(_ln_body, alpha=alpha),
        out_shape=(jax.ShapeDtypeStruct((t, d), F32), jax.ShapeDtypeStruct((t, d), BF16)),
        grid=(t // tr,),
        in_specs=[row, row, vec, vec],
        out_specs=(row, row),
        compiler_params=_params("parallel"),
        name=name,
    )(h, mix, g.reshape(1, d), b.reshape(1, d))


def _hgrn_chunk(z, q_raw, v, lb, st, lvl, row, backward):
    c_len = z.shape[0]
    e = jnp.exp(-jnp.abs(z))
    log_sig = jnp.minimum(z, 0.0) - jnp.log1p(e)
    sig_neg = jnp.where(z >= 0.0, e, 1.0) / (1.0 + e)
    a = jnp.log(lb)
    c = jnp.log1p(-lb) + log_sig
    g = jnp.maximum(a, c) + jnp.log1p(jnp.exp(-jnp.abs(a - c)))
    k = (1.0 - lb) * sig_neg
    q = q_raw * _sigmoid(q_raw)

    b = g
    for j in range(HGRN_LEVELS):
        sh = 1 << j
        b = b + jnp.where(row >= sh, pltpu.roll(b, sh, 0), 0.0)
    bx = b - g

    v_bf = v.astype(BF16)
    end, start = b, bx
    scores = jnp.zeros((c_len, c_len), F32)
    for m in range(HGRN_LEVELS):
        if backward:
            q_exp, k_exp = end - bx, bx - start
        else:
            q_exp, k_exp = b - start, end - b
        q_m = (q * jnp.exp(q_exp)).astype(BF16)
        k_m = (k * jnp.exp(k_exp)).astype(BF16)
        p_m = lax.dot_general(q_m, k_m, (((1,), (1,)), ((), ())), preferred_element_type=F32)
        scores = jnp.where(lvl == m + 1, p_m, scores)
        sh = 1 << m
        upper = (row & sh) != 0
        end = jnp.where(upper, end, pltpu.roll(end, c_len - sh, 0))
        start = jnp.where(upper, pltpu.roll(start, sh, 0), start)
    if backward:
        q_exp, k_exp = end - bx, bx - start
    else:
        q_exp, k_exp = b - start, end - b
    q_c = (q * jnp.exp(q_exp)).astype(BF16)
    k_c = (k * jnp.exp(k_exp)).astype(BF16)
    decay = jnp.exp(end[0:1, :] - start[0:1, :])

    diag = jnp.sum(q * k, axis=1, keepdims=True)
    o = (jnp.dot(scores.astype(BF16), v_bf, preferred_element_type=F32)
         + diag * v
         + lax.dot_general(q_c, st.astype(BF16), (((1,), (1,)), ((), ())), preferred_element_type=F32))
    st_new = st * decay + lax.dot_general(v_bf, k_c, (((0,), (0,)), ((), ())), preferred_element_type=F32)
    return o, st_new


def _hgrn_body(qf_ref, zf_ref, vf_ref, qb_ref, zb_ref, vb_ref, lb_ref, of_ref, ob_ref, st_ref, *, n_chunks):
    @pl.when(pl.program_id(2) == 0)
    def _():
        st_ref[...] = jnp.zeros_like(st_ref)

    c_len = HGRN_CHUNK
    row = lax.broadcasted_iota(jnp.int32, (c_len, c_len), 0)
    col = lax.broadcasted_iota(jnp.int32, (c_len, c_len), 1)
    lvl = 32 - lax.clz(row ^ col)
    lvl_f = jnp.where(row > col, lvl, 0)
    lvl_b = jnp.where(row < col, lvl, 0)
    lb_f = lb_ref[0]
    lb_b = lb_ref[1]

    def step(c, carry):
        rf = pl.ds(pl.multiple_of(c * c_len, c_len), c_len)
        rb = pl.ds(pl.multiple_of((n_chunks - 1 - c) * c_len, c_len), c_len)
        o_f, st_f = _hgrn_chunk(zf_ref[rf, :], qf_ref[rf, :], vf_ref[rf, :], lb_f, st_ref[0], lvl_f, row, False)
        of_ref[rf, :] = o_f
        st_ref[0] = st_f
        o_b, st_b = _hgrn_chunk(zb_ref[rb, :], qb_ref[rb, :], vb_ref[rb, :], lb_b, st_ref[1], lvl_b, row, True)
        ob_ref[rb, :] = o_b
        st_ref[1] = st_b
        return carry

    lax.fori_loop(0, n_chunks, step, 0)


def _hgrn_scan(proj, lb, *, batch, seq, width, tb, name):
    heads = width // HEAD_DIM
    tb = _pick(seq, tb)
    nb = seq // tb
    n_chunks = tb // HGRN_CHUNK

    def spec(col0, reverse):
        def imap(b, h, t):
            tt = (nb - 1 - t) if reverse else t
            return (b * nb + tt, col0 * heads + h)
        return pl.BlockSpec((tb, HEAD_DIM), imap)

    out_f = pl.BlockSpec((tb, HEAD_DIM), lambda b, h, t: (b * nb + t, h))
    out_b = pl.BlockSpec((tb, HEAD_DIM), lambda b, h, t: (b * nb + (nb - 1 - t), h))
    t_rows = batch * seq
    return pl.pallas_call(
        functools.partial(_hgrn_body, n_chunks=n_chunks),
        out_shape=(jax.ShapeDtypeStruct((t_rows, width), F32), jax.ShapeDtypeStruct((t_rows, width), F32)),
        grid=(batch, heads, nb),
        in_specs=[spec(0, False), spec(1, False), spec(3, False),
                  spec(0, True), spec(2, True), spec(3, True),
                  pl.BlockSpec((2, 1, HEAD_DIM), lambda b, h, t: (0, 0, h))],
        out_specs=(out_f, out_b),
        scratch_shapes=[pltpu.VMEM((2, HEAD_DIM, HEAD_DIM), F32)],
        compiler_params=_params("parallel", "parallel", "arbitrary"),
        name=name,
    )(proj, proj, proj, proj, proj, proj, lb.reshape(2, 1, width))


def _hgrn_out_body(of_ref, ob_ref, g_ref, ng_ref, o_ref, *, heads):
    for h in range(heads):
        sl = slice(h * HEAD_DIM, (h + 1) * HEAD_DIM)
        o = of_ref[:, sl] + ob_ref[:, sl]
        o = o * lax.rsqrt(jnp.mean(o * o, axis=-1, keepdims=True) + RMS_EPS)
        gate = g_ref[:, sl]
        o_ref[:, sl] = (o * ng_ref[:, sl] * (gate * _sigmoid(gate))).astype(o_ref.dtype)


def _hgrn_out(o_f, o_b, proj, norm_g, *, tr, name):
    t, width = o_f.shape
    tr = _pick(t, tr)
    row = pl.BlockSpec((tr, width), lambda i: (i, 0))
    return pl.pallas_call(
        functools.partial(_hgrn_out_body, heads=width // HEAD_DIM),
        out_shape=jax.ShapeDtypeStruct((t, width), BF16),
        grid=(t // tr,),
        in_specs=[row, row, pl.BlockSpec((tr, width), lambda i: (i, 4)),
                  pl.BlockSpec((1, width), lambda i: (0, 0))],
        out_specs=row,
        compiler_params=_params("parallel"),
        name=name,
    )(o_f, o_b, proj, norm_g.reshape(1, width))


def _rope_body(pos_ref, invf_ref, q_ref, k_ref, v_ref, qo_ref, ko_ref, vo_ref, *, q_heads, kv_heads, scale):
    half = ROPE_DIM // 2
    ang = pos_ref[...].astype(F32) * invf_ref[...]
    lane = lax.broadcasted_iota(jnp.int32, ang.shape, 1)
    cos = jnp.where(lane < ROPE_DIM, jnp.cos(ang), 1.0)
    sin = jnp.sin(ang)
    sin = jnp.where(lane < half, -sin, jnp.where(lane < ROPE_DIM, sin, 0.0))

    def rot(x):
        partner = jnp.where(lane < half, pltpu.roll(x, LANES - half, 1), pltpu.roll(x, half, 1))
        return x * cos + partner * sin

    for h in range(q_heads):
        sl = slice(h * HEAD_DIM, (h + 1) * HEAD_DIM)
        qo_ref[:, sl] = (rot(q_ref[:, sl]) * scale).astype(BF16)
    for h in range(kv_heads):
        sl = slice(h * HEAD_DIM, (h + 1) * HEAD_DIM)
        ko_ref[:, sl] = rot(k_ref[:, sl]).astype(BF16)
    vo_ref[...] = v_ref[...].astype(BF16)


def _rope(proj, positions, *, q_col, q_width, kv_width, tr, name):
    t = proj.shape[0]
    tr = _pick(t, tr)
    half = ROPE_DIM // 2
    inv_freq = jnp.power(jnp.float32(ROPE_THETA), -jnp.arange(half, dtype=F32) * (2.0 / ROPE_DIM))
    invf = jnp.zeros((1, LANES), F32).at[0, :ROPE_DIM].set(jnp.concatenate([inv_freq, inv_freq]))
    k_blk = (q_col + q_width) // kv_width
    return pl.pallas_call(
        functools.partial(_rope_body, q_heads=q_width // HEAD_DIM, kv_heads=kv_width // HEAD_DIM,
                          scale=HEAD_DIM ** -0.5),
        out_shape=(jax.ShapeDtypeStruct((t, q_width), BF16), jax.ShapeDtypeStruct((t, kv_width), BF16),
                   jax.ShapeDtypeStruct((t, kv_width), BF16)),
        grid=(t // tr,),
        in_specs=[pl.BlockSpec((tr, 1), lambda i: (i, 0)),
                  pl.BlockSpec((1, LANES), lambda i: (0, 0)),
                  pl.BlockSpec((tr, q_width), lambda i: (i, q_col // q_width)),
                  pl.BlockSpec((tr, kv_width), lambda i: (i, k_blk)),
                  pl.BlockSpec((tr, kv_width), lambda i: (i, k_blk + 1))],
        out_specs=(pl.BlockSpec((tr, q_width), lambda i: (i, 0)),
                   pl.BlockSpec((tr, kv_width), lambda i: (i, 0)),
                   pl.BlockSpec((tr, kv_width), lambda i: (i, 0))),
        compiler_params=_params("parallel"),
        name=name,
    )(positions.reshape(t, 1), invf, proj, proj, proj)


def _swa_body(sink_ref, q_ref, kp_ref, kc_ref, kn_ref, vp_ref, vc_ref, vn_ref, o_ref, *, n_blocks):
    h = pl.program_id(1)
    n = pl.program_id(2)
    blk = ATTN_BLOCK
    q = jnp.concatenate([q_ref[:, g * HEAD_DIM:(g + 1) * HEAD_DIM] for g in range(ATTN_GROUP)], axis=0)
    k = jnp.concatenate([kp_ref[...], kc_ref[...], kn_ref[...]], axis=0)
    v = jnp.concatenate([vp_ref[...], vc_ref[...], vn_ref[...]], axis=0)
    s = lax.dot_general(q, k, (((1,), (1,)), ((), ())), preferred_element_type=F32)
    i = lax.broadcasted_iota(jnp.int32, s.shape, 0) & (blk - 1)
    j = lax.broadcasted_iota(jnp.int32, s.shape, 1)
    rel = j - i
    valid = (rel >= blk - WINDOW) & (rel <= blk + WINDOW)
    valid &= (j >= blk) | (n > 0)
    valid &= (j < 2 * blk) | (n < n_blocks - 1)
    s = jnp.where(valid, s, -jnp.inf)
    r = lax.broadcasted_iota(jnp.int32, (ATTN_GROUP * blk, 1), 0)
    sink = jnp.zeros((ATTN_GROUP * blk, 1), F32)
    for g in range(ATTN_GROUP):
        sink = jnp.where(r >= g * blk, sink_ref[h * ATTN_GROUP + g], sink)
    m = jnp.maximum(jnp.max(s, axis=-1, keepdims=True), sink)
    p = jnp.exp(s - m)
    denom = jnp.sum(p, axis=-1, keepdims=True) + jnp.exp(sink - m)
    o = jnp.dot(p.astype(BF16), v, preferred_element_type=F32) / denom
    for g in range(ATTN_GROUP):
        o_ref[:, g * HEAD_DIM:(g + 1) * HEAD_DIM] = o[g * blk:(g + 1) * blk, :].astype(o_ref.dtype)


def _window_attention(q, k, v, sink, *, batch, seq, name):
    t, q_width = q.shape
    kv_heads = k.shape[1] // HEAD_DIM
    nb = seq // ATTN_BLOCK
    gw = ATTN_GROUP * HEAD_DIM

    def kv_spec(shift):
        def imap(b, h, n):
            return (b * nb + jnp.clip(n + shift, 0, nb - 1), h)
        return pl.BlockSpec((ATTN_BLOCK, HEAD_DIM), imap)

    q_spec = pl.BlockSpec((ATTN_BLOCK, gw), lambda b, h, n: (b * nb + n, h))
    return pl.pallas_call(
        functools.partial(_swa_body, n_blocks=nb),
        out_shape=jax.ShapeDtypeStruct((t, q_width), BF16),
        grid=(batch, kv_heads, nb),
        in_specs=[pl.BlockSpec(memory_space=pltpu.SMEM), q_spec,
                  kv_spec(-1), kv_spec(0), kv_spec(1), kv_spec(-1), kv_spec(0), kv_spec(1)],
        out_specs=q_spec,
        compiler_params=_params("parallel", "parallel", "arbitrary"),
        name=name,
    )(sink, q, k, k, k, v, v, v)


def _xattn_body(a_ref, wq_ref, k_ref, v_ref, o_ref, *, scale):
    q = jnp.dot(a_ref[...], wq_ref[...], preferred_element_type=F32) * scale
    s = lax.dot_general(q.astype(BF16), k_ref[...], (((1,), (1,)), ((), ())), preferred_element_type=F32)
    m = jnp.max(s, axis=-1, keepdims=True)
    p = jnp.exp(s - m)
    denom = jnp.sum(p, axis=-1, keepdims=True)
    o = jnp.dot(p.astype(BF16), v_ref[...], preferred_element_type=F32) / denom
    o_ref[...] = o.astype(o_ref.dtype)


def _cross_attention(a, wq, k, v, *, batch, seq, mem, tm, name):
    t, d = a.shape
    hd = d // MEM_HEADS
    tm = _pick(seq, tm)
    nb = seq // tm
    return pl.pallas_call(
        functools.partial(_xattn_body, scale=hd ** -0.5),
        out_shape=jax.ShapeDtypeStruct((t, d), BF16),
        grid=(batch, nb, MEM_HEADS),
        in_specs=[pl.BlockSpec((tm, d), lambda b, i, h: (b * nb + i, 0)),
                  pl.BlockSpec((d, hd), lambda b, i, h: (0, h)),
                  pl.BlockSpec((mem, hd), lambda b, i, h: (b, h)),
                  pl.BlockSpec((mem, hd), lambda b, i, h: (b, h))],
        out_specs=pl.BlockSpec((tm, hd), lambda b, i, h: (b * nb + i, h)),
        compiler_params=_params("parallel", "parallel", "arbitrary"),
        name=name,
    )(a, wq, k, v)


def _sgu_body(u_ref, v_ref, g_ref, b_ref, ws_ref, bs_ref, o_ref, *, groups):
    v = v_ref[...]
    mu = jnp.mean(v, axis=-1, keepdims=True)
    vc = v - mu
    var = jnp.mean(vc * vc, axis=-1, keepdims=True)
    vn = (vc * lax.rsqrt(var + LN_EPS) * g_ref[...] + b_ref[...]).astype(BF16)
    for g in range(groups):
        sl = slice(g * LANES, (g + 1) * LANES)
        s = jnp.dot(ws_ref[g], vn[:, sl], preferred_element_type=F32) + bs_ref[:, sl]
        o_ref[:, sl] = (u_ref[:, sl] * s).astype(o_ref.dtype)


def _spatial_gating(z, norm_g, norm_b, w_s, b_s, *, name):
    t, w2 = z.shape
    w = w2 // 2
    groups = w // LANES
    row_u = pl.BlockSpec((GMLP_CHUNK, w), lambda i: (i, 0))
    row_v = pl.BlockSpec((GMLP_CHUNK, w), lambda i: (i, 1))
    vec = pl.BlockSpec((1, w), lambda i: (0, 0))
    bs = jnp.repeat(b_s.T, LANES, axis=1)
    return pl.pallas_call(
        functools.partial(_sgu_body, groups=groups),
        out_shape=jax.ShapeDtypeStruct((t, w), BF16),
        grid=(t // GMLP_CHUNK,),
        in_specs=[row_u, row_v, vec, vec,
                  pl.BlockSpec((groups, GMLP_CHUNK, GMLP_CHUNK), lambda i: (0, 0, 0)),
                  pl.BlockSpec((GMLP_CHUNK, w), lambda i: (0, 0))],
        out_specs=row_u,
        compiler_params=_params("parallel"),
        name=name,
    )(z, z, norm_g.reshape(1, w), norm_b.reshape(1, w), w_s.astype(BF16), bs)


def kernel(x, mem, positions, ab_w_in, hgrn_lower_bounds, hgrn_norm_g, attn_sink, ab_w_out,
           gmlp_w_in, gmlp_norm_g, gmlp_norm_b, gmlp_w_spatial, gmlp_b_spatial, gmlp_w_out,
           xa_w_q, xa_w_k, xa_w_v, xa_w_o, ffn_w_gate, ffn_w_up, ffn_w_down, ln_g, ln_b):
    batch, seq, d = x.shape
    n_mem = mem.shape[1]
    depth = ln_g.shape[0]
    t = batch * seq
    alpha = (2.0 * depth) ** 0.25
    hgrn_w = d // 2
    attn_w = d - hgrn_w
    kv_w = attn_w // ATTN_GROUP

    lb_all = jnp.cumsum(jax.nn.softmax(hgrn_lower_bounds.astype(F32), axis=1), axis=1)

    h = x.reshape(t, d)
    h_bf = h.astype(BF16)
    mem_bf = mem.reshape(batch * n_mem, d).astype(BF16)

    for layer in range(depth):
        j = layer // 2
        tag = "l%d_" % layer
        if layer % 2 == 0:
            proj = _matmul([(h_bf, ab_w_in[j].astype(BF16))], F32, tm=1024, tn=1024, name=tag + "mix_in")
            o_f, o_b = _hgrn_scan(proj, lb_all[:, layer], batch=batch, seq=seq, width=hgrn_w, tb=512,
                                  name=tag + "hgrn_scan")
            a_out = _hgrn_out(o_f, o_b, proj, hgrn_norm_g[j], tr=256, name=tag + "hgrn_out")
            q_r, k_r, v_r = _rope(proj, positions, q_col=5 * hgrn_w, q_width=attn_w, kv_width=kv_w, tr=256,
                                  name=tag + "rope")
            b_out = _window_attention(q_r, k_r, v_r, attn_sink[j].astype(F32), batch=batch, seq=seq,
                                      name=tag + "swa")
            w_out = ab_w_out[j].astype(BF16)
            mix = _matmul([(a_out, w_out[:hgrn_w]), (b_out, w_out[hgrn_w:])], F32, tm=1024, tn=1024,
                          name=tag + "mix_out")
        else:
            z = _matmul([(h_bf, gmlp_w_in[j].astype(BF16))], F32, tm=1024, tn=1024, epilogue=_gelu_exact,
                        name=tag + "gmlp_in")
            gated = _spatial_gating(z, gmlp_norm_g[j], gmlp_norm_b[j], gmlp_w_spatial[j], gmlp_b_spatial[j],
                                    name=tag + "gmlp_sgu")
            mix = _matmul([(gated, gmlp_w_out[j].astype(BF16))], F32, tm=1024, tn=1024, name=tag + "mix_out")
        h, h_bf = _residual_layer_norm(h, mix, ln_g[layer, 0], ln_b[layer, 0], alpha=alpha, tr=256,
                                       name=tag + "ln0")

        k_m = _matmul([(mem_bf, xa_w_k[layer])], BF16, tm=512, tn=512, name=tag + "xa_k")
        v_m = _matmul([(mem_bf, xa_w_v[layer])], BF16, tm=512, tn=512, name=tag + "xa_v")
        att = _cross_attention(h_bf, xa_w_q[layer].astype(BF16), k_m, v_m, batch=batch, seq=seq, mem=n_mem,
                               tm=512, name=tag + "xa_attn")
        xa = _matmul([(att, xa_w_o[layer].astype(BF16))], F32, tm=1024, tn=1024, name=tag + "xa_out")
        h, h_bf = _residual_layer_norm(h, xa, ln_g[layer, 1], ln_b[layer, 1], alpha=alpha, tr=256,
                                       name=tag + "ln1")

        act = _ffn_up(h_bf, ffn_w_gate[layer].astype(BF16), ffn_w_up[layer].astype(BF16), tm=2048, tn=256,
                      name=tag + "ffn_up")
        ff = _matmul([(act, ffn_w_down[layer].astype(BF16))], F32, tm=512, tn=512, name=tag + "ffn_down")
        h, h_bf = _residual_layer_norm(h, ff, ln_g[layer, 2], ln_b[layer, 2], alpha=alpha, tr=256,
                                       name=tag + "ln2")
    return h.reshape(batch, seq, d)
```

```python
import functools
import math

import jax
import jax.numpy as jnp
from jax import lax
from jax.experimental import pallas as pl
from jax.experimental.pallas import tpu as pltpu

F32 = jnp.float32
BF16 = jnp.bfloat16

LANES = 128
SUBLANES = 8
HEAD_DIM = LANES
VMEM_LIMIT_BYTES = 58 * 1024 * 1024

HGRN_CHUNK = 128
HGRN_LEVELS = 7
LOG2_E = 1.4426950408889634
WINDOW = 128
ATTN_BLOCK = 128
ATTN_GROUP = 4
ROPE_THETA = 500000.0
ROPE_DIM = HEAD_DIM // 4
GMLP_CHUNK = 128
MEM_HEADS = 4
LN_EPS = 1e-5
RMS_EPS = 1e-6
CAST_BLOCK_ELEMS = 1024 * 1024


def _params(*sem):
    return pltpu.CompilerParams(dimension_semantics=sem, vmem_limit_bytes=VMEM_LIMIT_BYTES)


def _pick(n, pref):
    t = min(pref, n)
    while n % t:
        t -= LANES
    return t


def _sigmoid(x):
    return 1.0 / (1.0 + jnp.exp(-x))


def _layer_norm_rows(y, g, b):
    mu = jnp.mean(y, axis=-1, keepdims=True)
    yc = y - mu
    var = jnp.mean(yc * yc, axis=-1, keepdims=True)
    return yc * lax.rsqrt(var + LN_EPS) * g + b


def _cast_body(x_ref, o_ref):
    o_ref[...] = x_ref[...].astype(o_ref.dtype)


def _cast_bf16(w, *, name):
    r, c = w.shape
    tr = _pick(r, max(LANES, (CAST_BLOCK_ELEMS // c) // LANES * LANES))
    spec = pl.BlockSpec((tr, c), lambda i: (i, 0))
    return pl.pallas_call(
        _cast_body,
        out_shape=jax.ShapeDtypeStruct((r, c), BF16),
        grid=(r // tr,),
        in_specs=[spec],
        out_specs=spec,
        compiler_params=_params("parallel"),
        name=name,
    )(w)


def _identity(x):
    return x


def _gelu_exact(x):
    return 0.5 * x * (1.0 + lax.erf(x * (1.0 / math.sqrt(2.0))))


def _dot_pairs(refs, n_pairs):
    acc = None
    for p in range(n_pairs):
        d = jnp.dot(refs[2 * p][...], refs[2 * p + 1][...].astype(BF16), preferred_element_type=F32)
        acc = d if acc is None else acc + d
    return acc


def _mm_body(*refs, n_pairs, epilogue):
    o_ref = refs[-1]
    o_ref[...] = epilogue(_dot_pairs(refs, n_pairs)).astype(o_ref.dtype)


def _pair_specs(pairs, tm, tn, a_once):
    in_specs, args = [], []
    a_mode = {"pipeline_mode": pl.Buffered(1)} if a_once else {}
    for a, w, kb in pairs:
        k = a.shape[1]
        in_specs += [pl.BlockSpec((tm, k), lambda i, j: (i, 0), **a_mode),
                     pl.BlockSpec((k, tn), lambda i, j, kb=kb: (kb, j))]
        args += [a, w]
    return in_specs, args


def _matmul(pairs, out_dtype, *, tm, tn, epilogue=_identity, a_once=False, name):
    m = pairs[0][0].shape[0]
    n = pairs[0][1].shape[1]
    tm = _pick(m, tm)
    tn = _pick(n, tn)
    in_specs, args = _pair_specs(pairs, tm, tn, a_once)
    return pl.pallas_call(
        functools.partial(_mm_body, n_pairs=len(pairs), epilogue=epilogue),
        out_shape=jax.ShapeDtypeStruct((m, n), out_dtype),
        grid=(m // tm, n // tn),
        in_specs=in_specs,
        out_specs=pl.BlockSpec((tm, tn), lambda i, j: (i, j)),
        compiler_params=_params("parallel", "arbitrary"),
        name=name,
    )(*args)


def _mm_ln_body(*refs, n_pairs, alpha, n_col, tn, slab):
    res_ref, g_ref, b_ref, of_ref, ob_ref, y_ref = refs[2 * n_pairs:]
    i = pl.program_id(0)
    j = pl.program_id(1)

    @pl.when((i == 0) & (j == 0))
    def _():
        y_ref[...] = jnp.zeros_like(y_ref)

    prev = (i + 1) % 2
    rows = pl.ds(pl.multiple_of(j * slab, slab), slab)
    tiles = [y_ref[prev, c, rows, :] for c in range(n_col)]
    inv_n = 1.0 / (n_col * tn)
    mu = sum(jnp.sum(y, axis=-1, keepdims=True) for y in tiles) * inv_n
    var = sum(jnp.sum((y - mu) * (y - mu), axis=-1, keepdims=True) for y in tiles) * inv_n
    rstd = lax.rsqrt(var + LN_EPS)
    for c in range(n_col):
        sl = slice(c * tn, (c + 1) * tn)
        o = (tiles[c] - mu) * rstd * g_ref[:, sl] + b_ref[:, sl]
        of_ref[:, sl] = o
        ob_ref[:, sl] = o.astype(BF16)
    y_ref[i % 2, j] = alpha * res_ref[...] + _dot_pairs(refs, n_pairs)


def _matmul_ln(pairs, res, g, b, *, alpha, tm, tn, name):
    m, n = res.shape
    tm = _pick(m, tm)
    tn = _pick(n, tn)
    n_row, n_col = m // tm, n // tn
    slab = tm // n_col

    def row_tile(i):
        return jnp.minimum(i, n_row - 1)

    def col_tile(i, j):
        return jnp.where(i < n_row, j, n_col - 1)

    in_specs, args = [], []
    for a, w, kb in pairs:
        k = a.shape[1]
        in_specs += [pl.BlockSpec((tm, k), lambda i, j: (row_tile(i), 0)),
                     pl.BlockSpec((k, tn), lambda i, j, kb=kb: (kb, col_tile(i, j)))]
        args += [a, w]
    vec = pl.BlockSpec((1, n), lambda i, j: (0, 0))
    out = pl.BlockSpec((slab, n), lambda i, j: (jnp.where(i == 0, 0, (i - 1) * n_col + j), 0))
    return pl.pallas_call(
        functools.partial(_mm_ln_body, n_pairs=len(pairs), alpha=alpha, n_col=n_col, tn=tn, slab=slab),
        out_shape=(jax.ShapeDtypeStruct((m, n), F32), jax.ShapeDtypeStruct((m, n), BF16)),
        grid=(n_row + 1, n_col),
        in_specs=in_specs + [pl.BlockSpec((tm, tn), lambda i, j: (row_tile(i), col_tile(i, j))), vec, vec],
        out_specs=(out, out),
        scratch_shapes=[pltpu.VMEM((2, n_col, tm, tn), F32)],
        compiler_params=_params("arbitrary", "arbitrary"),
        name=name,
    )(*args, res, g.reshape(1, n), b.reshape(1, n))


def _ffn_up_body(a_ref, wg_ref, wu_ref, o_ref):
    a = a_ref[...]
    g = jnp.dot(a, wg_ref[...].astype(BF16), preferred_element_type=F32)
    u = jnp.dot(a, wu_ref[...].astype(BF16), preferred_element_type=F32)
    o_ref[...] = (g * _sigmoid(g) * u).astype(o_ref.dtype)


def _ffn_up(a, wg, wu, *, tm, tn, name):
    m, k = a.shape
    n = wg.shape[1]
    tm = _pick(m, tm)
    tn = _pick(n, tn)
    return pl.pallas_call(
        _ffn_up_body,
        out_shape=jax.ShapeDtypeStruct((m, n), BF16),
        grid=(m // tm, n // tn),
        in_specs=[pl.BlockSpec((tm, k), lambda i, j: (i, 0), pipeline_mode=pl.Buffered(1)),
                  pl.BlockSpec((k, tn), lambda i, j: (0, j)),
                  pl.BlockSpec((k, tn), lambda i, j: (0, j))],
        out_specs=pl.BlockSpec((tm, tn), lambda i, j: (i, j)),
        compiler_params=_params("parallel", "arbitrary"),
        name=name,
    )(a, wg, wu)


def _ln_body(h_ref, m_ref, g_ref, b_ref, of_ref, ob_ref, *, alpha):
    o = _layer_norm_rows(alpha * h_ref[...] + m_ref[...], g_ref[...], b_ref[...])
    of_ref[...] = o
    ob_ref[...] = o.astype(BF16)


def _residual_layer_norm(h, mix, g, b, *, alpha, tr, name):
    t, d = h.shape
    tr = _pick(t, tr)
    row = pl.BlockSpec((tr, d), lambda i: (i, 0))
    vec = pl.BlockSpec((1, d), lambda i: (0, 0))
    return pl.pallas_call(
        functools.partial(_ln_body, alpha=alpha),
        out_shape=(jax.ShapeDtypeStruct((t, d), F32), jax.ShapeDtypeStruct((t, d), BF16)),
        grid=(t // tr,),
        in_specs=[row, row, vec, vec],
        out_specs=(row, row),
        compiler_params=_params("parallel"),
        name=name,
    )(h, mix, g.reshape(1, d), b.reshape(1, d))


def _hgrn_chunk(z, q_raw, v, lb, st, lvl, sub, backward):
    c_len = z.shape[0]
    groups = c_len // SUBLANES
    shape3 = (groups, SUBLANES, LANES)
    e = jnp.exp(-jnp.abs(z))
    log_sig = jnp.minimum(z, 0.0) - jnp.log(1.0 + e)
    sig_neg = jnp.where(z >= 0.0, e, 1.0) / (1.0 + e)
    a = jnp.log(lb)
    c = jnp.log1p(-lb) + log_sig
    g = (jnp.maximum(a, c) + jnp.log(1.0 + jnp.exp(-jnp.abs(a - c)))) * LOG2_E
    k = ((1.0 - lb) * sig_neg).reshape(shape3)
    q = (q_raw * _sigmoid(q_raw)).reshape(shape3)
    g = g.reshape(shape3)

    loc = g
    for j in range(3):
        sh = 1 << j
        loc = loc + jnp.where(sub >= sh, pltpu.roll(loc, sh, 1), 0.0)
    locx = loc - g
    tot = jnp.broadcast_to(loc[:, SUBLANES - 1:SUBLANES, :], shape3)
    base = [jnp.zeros((SUBLANES, LANES), F32)]
    for r in range(groups):
        base.append(base[r] + tot[r])
    b = loc + jnp.stack(base[:groups])
    bx = b - g

    def bounds(m):
        span = 1 << (m - 3)
        first = [(r // span) * span for r in range(groups)]
        return (jnp.stack([base[f] for f in first]), jnp.stack([base[f + span] for f in first]))

    def factors(q_exp, k_exp):
        q_m = (q * jnp.exp2(q_exp)).reshape(c_len, LANES).astype(BF16)
        k_m = (k * jnp.exp2(k_exp)).reshape(c_len, LANES).astype(BF16)
        return q_m, k_m

    scores = jnp.zeros((c_len, c_len), F32)
    end_l, start_l = loc, locx
    for m in range(HGRN_LEVELS):
        if m < 3:
            q_exp, k_exp = (end_l - locx, locx - start_l) if backward else (loc - start_l, end_l - loc)
            sh = 1 << m
            upper = (sub & sh) != 0
            end_l = jnp.where(upper, end_l, pltpu.roll(end_l, SUBLANES - sh, 1))
            start_l = jnp.where(upper, pltpu.roll(start_l, sh, 1), start_l)
        else:
            start, end = bounds(m)
            q_exp, k_exp = (end - bx, bx - start) if backward else (b - start, end - b)
        q_m, k_m = factors(q_exp, k_exp)
        p_m = lax.dot_general(q_m, k_m, (((1,), (1,)), ((), ())), preferred_element_type=F32)
        scores = jnp.where(lvl == m + 1, p_m, scores)
    total = base[groups]
    q_c, k_c = factors(total - bx, bx) if backward else factors(b, total - b)
    decay = jnp.exp2(total[0:1, :])

    q2 = q.reshape(c_len, LANES)
    k2 = k.reshape(c_len, LANES)
    v_bf = v.astype(BF16)
    diag = jnp.sum(q2 * k2, axis=1, keepdims=True)
    o = (jnp.dot(scores.astype(BF16), v_bf, preferred_element_type=F32)
         + diag * v
         + lax.dot_general(q_c, st.astype(BF16), (((1,), (1,)), ((), ())), preferred_element_type=F32))
    st_new = st * decay + lax.dot_general(v_bf, k_c, (((0,), (0,)), ((), ())), preferred_element_type=F32)
    return o, st_new


def _hgrn_body(qf_ref, zf_ref, vf_ref, qb_ref, zb_ref, vb_ref, lb_ref, of_ref, ob_ref, st_ref, *, n_chunks, heads):
    @pl.when(pl.program_id(2) == 0)
    def _():
        st_ref[...] = jnp.zeros_like(st_ref)

    c_len = HGRN_CHUNK
    row = lax.broadcasted_iota(jnp.int32, (c_len, c_len), 0)
    col = lax.broadcasted_iota(jnp.int32, (c_len, c_len), 1)
    lvl = 32 - lax.clz(row ^ col)
    lvl_f = jnp.where(row > col, lvl, 0)
    lvl_b = jnp.where(row < col, lvl, 0)
    sub = lax.broadcasted_iota(jnp.int32, (c_len // SUBLANES, SUBLANES, LANES), 1)

    def step(c, carry):
        rf = pl.ds(pl.multiple_of(c * c_len, c_len), c_len)
        rb = pl.ds(pl.multiple_of((n_chunks - 1 - c) * c_len, c_len), c_len)
        for h in range(heads):
            sl = slice(h * HEAD_DIM, (h + 1) * HEAD_DIM)
            o_f, st_f = _hgrn_chunk(zf_ref[rf, sl], qf_ref[rf, sl], vf_ref[rf, sl], lb_ref[0, :, sl],
                                    st_ref[0, h], lvl_f, sub, False)
            of_ref[rf, sl] = o_f
            st_ref[0, h] = st_f
            o_b, st_b = _hgrn_chunk(zb_ref[rb, sl], qb_ref[rb, sl], vb_ref[rb, sl], lb_ref[1, :, sl],
                                    st_ref[1, h], lvl_b, sub, True)
            ob_ref[rb, sl] = o_b
            st_ref[1, h] = st_b
        return carry

    lax.fori_loop(0, n_chunks, step, 0)


def _hgrn_scan(proj, lb, *, batch, seq, width, tb, heads_per_step, name):
    hw = heads_per_step * HEAD_DIM
    head_groups = width // hw
    tb = _pick(seq, tb)
    nb = seq // tb
    n_chunks = tb // HGRN_CHUNK

    def spec(col0, reverse):
        def imap(b, h, t):
            tt = (nb - 1 - t) if reverse else t
            return (b * nb + tt, col0 * head_groups + h)
        return pl.BlockSpec((tb, hw), imap)

    out_f = pl.BlockSpec((tb, hw), lambda b, h, t: (b * nb + t, h))
    out_b = pl.BlockSpec((tb, hw), lambda b, h, t: (b * nb + (nb - 1 - t), h))
    t_rows = batch * seq
    return pl.pallas_call(
        functools.partial(_hgrn_body, n_chunks=n_chunks, heads=heads_per_step),
        out_shape=(jax.ShapeDtypeStruct((t_rows, width), F32), jax.ShapeDtypeStruct((t_rows, width), F32)),
        grid=(batch, head_groups, nb),
        in_specs=[spec(0, False), spec(1, False), spec(3, False),
                  spec(0, True), spec(2, True), spec(3, True),
                  pl.BlockSpec((2, 1, hw), lambda b, h, t: (0, 0, h))],
        out_specs=(out_f, out_b),
        scratch_shapes=[pltpu.VMEM((2, heads_per_step, HEAD_DIM, HEAD_DIM), F32)],
        compiler_params=_params("parallel", "parallel", "arbitrary"),
        name=name,
    )(proj, proj, proj, proj, proj, proj, lb.reshape(2, 1, width))


def _hgrn_out_body(of_ref, ob_ref, g_ref, ng_ref, o_ref, *, heads):
    for h in range(heads):
        sl = slice(h * HEAD_DIM, (h + 1) * HEAD_DIM)
        o = of_ref[:, sl] + ob_ref[:, sl]
        o = o * lax.rsqrt(jnp.mean(o * o, axis=-1, keepdims=True) + RMS_EPS)
        gate = g_ref[:, sl]
        o_ref[:, sl] = (o * ng_ref[:, sl] * (gate * _sigmoid(gate))).astype(o_ref.dtype)


def _hgrn_out(o_f, o_b, proj, norm_g, *, tr, name):
    t, width = o_f.shape
    tr = _pick(t, tr)
    row = pl.BlockSpec((tr, width), lambda i: (i, 0))
    return pl.pallas_call(
        functools.partial(_hgrn_out_body, heads=width // HEAD_DIM),
        out_shape=jax.ShapeDtypeStruct((t, width), BF16),
        grid=(t // tr,),
        in_specs=[row, row, pl.BlockSpec((tr, width), lambda i: (i, 4)),
                  pl.BlockSpec((1, width), lambda i: (0, 0))],
        out_specs=row,
        compiler_params=_params("parallel"),
        name=name,
    )(o_f, o_b, proj, norm_g.reshape(1, width))


def _rope_body(pos_ref, invf_ref, q_ref, k_ref, v_ref, qo_ref, ko_ref, vo_ref, *, q_heads, kv_heads, scale):
    half = ROPE_DIM // 2
    ang = pos_ref[...].astype(F32) * invf_ref[...]
    lane = lax.broadcasted_iota(jnp.int32, ang.shape, 1)
    cos = jnp.where(lane < ROPE_DIM, jnp.cos(ang), 1.0)
    sin = jnp.sin(ang)
    sin = jnp.where(lane < half, -sin, jnp.where(lane < ROPE_DIM, sin, 0.0))

    def rot(x):
        partner = jnp.where(lane < half, pltpu.roll(x, LANES - half, 1), pltpu.roll(x, half, 1))
        return x * cos + partner * sin

    for h in range(q_heads):
        sl = slice(h * HEAD_DIM, (h + 1) * HEAD_DIM)
        qo_ref[:, sl] = (rot(q_ref[:, sl]) * scale).astype(BF16)
    for h in range(kv_heads):
        sl = slice(h * HEAD_DIM, (h + 1) * HEAD_DIM)
        ko_ref[:, sl] = rot(k_ref[:, sl]).astype(BF16)
    vo_ref[...] = v_ref[...].astype(BF16)


def _rope(proj, positions, *, q_col, q_width, kv_width, tr, name):
    t = proj.shape[0]
    tr = _pick(t, tr)
    half = ROPE_DIM // 2
    inv_freq = jnp.power(jnp.float32(ROPE_THETA), -jnp.arange(half, dtype=F32) * (2.0 / ROPE_DIM))
    invf = jnp.zeros((1, LANES), F32).at[0, :ROPE_DIM].set(jnp.concatenate([inv_freq, inv_freq]))
    k_blk = (q_col + q_width) // kv_width
    return pl.pallas_call(
        functools.partial(_rope_body, q_heads=q_width // HEAD_DIM, kv_heads=kv_width // HEAD_DIM,
                          scale=HEAD_DIM ** -0.5),
        out_shape=(jax.ShapeDtypeStruct((t, q_width), BF16), jax.ShapeDtypeStruct((t, kv_width), BF16),
                   jax.ShapeDtypeStruct((t, kv_width), BF16)),
        grid=(t // tr,),
        in_specs=[pl.BlockSpec((tr, 1), lambda i: (i, 0)),
                  pl.BlockSpec((1, LANES), lambda i: (0, 0)),
                  pl.BlockSpec((tr, q_width), lambda i: (i, q_col // q_width)),
                  pl.BlockSpec((tr, kv_width), lambda i: (i, k_blk)),
                  pl.BlockSpec((tr, kv_width), lambda i: (i, k_blk + 1))],
        out_specs=(pl.BlockSpec((tr, q_width), lambda i: (i, 0)),
                   pl.BlockSpec((tr, kv_width), lambda i: (i, 0)),
                   pl.BlockSpec((tr, kv_width), lambda i: (i, 0))),
        compiler_params=_params("parallel"),
        name=name,
    )(positions.reshape(t, 1), invf, proj, proj, proj)


def _swa_body(sink_ref, q_ref, kp_ref, kc_ref, kn_ref, vp_ref, vc_ref, vn_ref, o_ref, *, n_blocks):
    h = pl.program_id(1)
    n = pl.program_id(2)
    blk = ATTN_BLOCK
    q = jnp.concatenate([q_ref[:, g * HEAD_DIM:(g + 1) * HEAD_DIM] for g in range(ATTN_GROUP)], axis=0)
    k = jnp.concatenate([kp_ref[...], kc_ref[...], kn_ref[...]], axis=0)
    v = jnp.concatenate([vp_ref[...], vc_ref[...], vn_ref[...]], axis=0)
    s = lax.dot_general(q, k, (((1,), (1,)), ((), ())), preferred_element_type=F32)
    i = lax.broadcasted_iota(jnp.int32, s.shape, 0) & (blk - 1)
    j = lax.broadcasted_iota(jnp.int32, s.shape, 1)
    rel = j - i
    valid = (rel >= blk - WINDOW) & (rel <= blk + WINDOW)
    valid &= (j >= blk) | (n > 0)
    valid &= (j < 2 * blk) | (n < n_blocks - 1)
    s = jnp.where(valid, s, -jnp.inf)
    r = lax.broadcasted_iota(jnp.int32, (ATTN_GROUP * blk, 1), 0)
    sink = jnp.zeros((ATTN_GROUP * blk, 1), F32)
    for g in range(ATTN_GROUP):
        sink = jnp.where(r >= g * blk, sink_ref[h * ATTN_GROUP + g], sink)
    m = jnp.maximum(jnp.max(s, axis=-1, keepdims=True), sink)
    p = jnp.exp(s - m)
    denom = jnp.sum(p, axis=-1, keepdims=True) + jnp.exp(sink - m)
    o = jnp.dot(p.astype(BF16), v, preferred_element_type=F32) / denom
    for g in range(ATTN_GROUP):
        o_ref[:, g * HEAD_DIM:(g + 1) * HEAD_DIM] = o[g * blk:(g + 1) * blk, :].astype(o_ref.dtype)


def _window_attention(q, k, v, sink, *, batch, seq, name):
    t, q_width = q.shape
    kv_heads = k.shape[1] // HEAD_DIM
    nb = seq // ATTN_BLOCK
    gw = ATTN_GROUP * HEAD_DIM

    def kv_spec(shift):
        def imap(b, h, n):
            return (b * nb + jnp.clip(n + shift, 0, nb - 1), h)
        return pl.BlockSpec((ATTN_BLOCK, HEAD_DIM), imap)

    q_spec = pl.BlockSpec((ATTN_BLOCK, gw), lambda b, h, n: (b * nb + n, h))
    return pl.pallas_call(
        functools.partial(_swa_body, n_blocks=nb),
        out_shape=jax.ShapeDtypeStruct((t, q_width), BF16),
        grid=(batch, kv_heads, nb),
        in_specs=[pl.BlockSpec(memory_space=pltpu.SMEM), q_spec,
                  kv_spec(-1), kv_spec(0), kv_spec(1), kv_spec(-1), kv_spec(0), kv_spec(1)],
        out_specs=q_spec,
        compiler_params=_params("parallel", "parallel", "arbitrary"),
        name=name,
    )(sink, q, k, k, k, v, v, v)


def _xattn_body(a_ref, wq_ref, k_ref, v_ref, o_ref, *, scale):
    q = jnp.dot(a_ref[...], wq_ref[...], preferred_element_type=F32) * scale
    s = lax.dot_general(q.astype(BF16), k_ref[...], (((1,), (1,)), ((), ())), preferred_element_type=F32)
    m = jnp.max(s, axis=-1, keepdims=True)
    p = jnp.exp(s - m)
    denom = jnp.sum(p, axis=-1, keepdims=True)
    o = jnp.dot(p.astype(BF16), v_ref[...], preferred_element_type=F32) / denom
    o_ref[...] = o.astype(o_ref.dtype)


def _cross_attention(a, wq, k, v, *, batch, seq, mem, tm, name):
    t, d = a.shape
    hd = d // MEM_HEADS
    tm = _pick(seq, tm)
    nb = seq // tm
    return pl.pallas_call(
        functools.partial(_xattn_body, scale=hd ** -0.5),
        out_shape=jax.ShapeDtypeStruct((t, d), BF16),
        grid=(batch, nb, MEM_HEADS),
        in_specs=[pl.BlockSpec((tm, d), lambda b, i, h: (b * nb + i, 0)),
                  pl.BlockSpec((d, hd), lambda b, i, h: (0, h)),
                  pl.BlockSpec((mem, hd), lambda b, i, h: (b, h)),
                  pl.BlockSpec((mem, hd), lambda b, i, h: (b, h))],
        out_specs=pl.BlockSpec((tm, hd), lambda b, i, h: (b * nb + i, h)),
        compiler_params=_params("parallel", "parallel", "arbitrary"),
        name=name,
    )(a, wq, k, v)


def _sgu_body(u_ref, v_ref, g_ref, b_ref, ws_ref, bs_ref, o_ref, *, groups):
    vn = _layer_norm_rows(v_ref[...], g_ref[...], b_ref[...]).astype(BF16)
    for g in range(groups):
        sl = slice(g * LANES, (g + 1) * LANES)
        s = jnp.dot(ws_ref[g], vn[:, sl], preferred_element_type=F32) + bs_ref[:, sl]
        o_ref[:, sl] = (u_ref[:, sl] * s).astype(o_ref.dtype)


def _spatial_gating(z, norm_g, norm_b, w_s, b_s, *, name):
    t, w2 = z.shape
    w = w2 // 2
    groups = w // LANES
    row_u = pl.BlockSpec((GMLP_CHUNK, w), lambda i: (i, 0))
    row_v = pl.BlockSpec((GMLP_CHUNK, w), lambda i: (i, 1))
    vec = pl.BlockSpec((1, w), lambda i: (0, 0))
    bs = jnp.repeat(b_s.T, LANES, axis=1)
    return pl.pallas_call(
        functools.partial(_sgu_body, groups=groups),
        out_shape=jax.ShapeDtypeStruct((t, w), BF16),
        grid=(t // GMLP_CHUNK,),
        in_specs=[row_u, row_v, vec, vec,
                  pl.BlockSpec((groups, GMLP_CHUNK, GMLP_CHUNK), lambda i: (0, 0, 0)),
                  pl.BlockSpec((GMLP_CHUNK, w), lambda i: (0, 0))],
        out_specs=row_u,
        compiler_params=_params("parallel"),
        name=name,
    )(z, z, norm_g.reshape(1, w), norm_b.reshape(1, w), w_s.astype(BF16), bs)


def kernel(x, mem, positions, ab_w_in, hgrn_lower_bounds, hgrn_norm_g, attn_sink, ab_w_out,
           gmlp_w_in, gmlp_norm_g, gmlp_norm_b, gmlp_w_spatial, gmlp_b_spatial, gmlp_w_out,
           xa_w_q, xa_w_k, xa_w_v, xa_w_o, ffn_w_gate, ffn_w_up, ffn_w_down, ln_g, ln_b):
    batch, seq, d = x.shape
    n_mem = mem.shape[1]
    depth = ln_g.shape[0]
    t = batch * seq
    alpha = (2.0 * depth) ** 0.25
    hgrn_w = d // 2
    attn_w = d - hgrn_w
    kv_w = attn_w // ATTN_GROUP

    lb_all = jnp.cumsum(jax.nn.softmax(hgrn_lower_bounds.astype(F32), axis=1), axis=1)

    h = x.reshape(t, d)
    h_bf = h.astype(BF16)
    mem_bf = mem.reshape(batch * n_mem, d).astype(BF16)

    for layer in range(depth):
        j = layer // 2
        tag = "l%d_" % layer
        if layer % 2 == 0:
            proj = _matmul([(h_bf, ab_w_in[j], 0)], F32, tm=2048, tn=512, a_once=True, name=tag + "mix_in")
            o_f, o_b = _hgrn_scan(proj, lb_all[:, layer], batch=batch, seq=seq, width=hgrn_w, tb=512,
                                  heads_per_step=2, name=tag + "hgrn_scan")
            a_out = _hgrn_out(o_f, o_b, proj, hgrn_norm_g[j], tr=256, name=tag + "hgrn_out")
            q_r, k_r, v_r = _rope(proj, positions, q_col=5 * hgrn_w, q_width=attn_w, kv_width=kv_w, tr=256,
                                  name=tag + "rope")
            b_out = _window_attention(q_r, k_r, v_r, attn_sink[j].astype(F32), batch=batch, seq=seq,
                                      name=tag + "swa")
            w_out = _cast_bf16(ab_w_out[j], name=tag + "cast_mix_out")
            mix_pairs = [(a_out, w_out, 0), (b_out, w_out, 1)]
        else:
            z = _matmul([(h_bf, gmlp_w_in[j], 0)], F32, tm=2048, tn=512, epilogue=_gelu_exact, a_once=True,
                        name=tag + "gmlp_in")
            gated = _spatial_gating(z, gmlp_norm_g[j], gmlp_norm_b[j], gmlp_w_spatial[j], gmlp_b_spatial[j],
                                    name=tag + "gmlp_sgu")
            mix_pairs = [(gated, _cast_bf16(gmlp_w_out[j], name=tag + "cast_mix_out"), 0)]
        h, h_bf = _matmul_ln(mix_pairs, h, ln_g[layer, 0], ln_b[layer, 0], alpha=alpha, tm=512, tn=512,
                             name=tag + "mix_out_ln0")

        k_m = _matmul([(mem_bf, xa_w_k[layer], 0)], BF16, tm=512, tn=512, name=tag + "xa_k")
        v_m = _matmul([(mem_bf, xa_w_v[layer], 0)], BF16, tm=512, tn=512, name=tag + "xa_v")
        att = _cross_attention(h_bf, _cast_bf16(xa_w_q[layer], name=tag + "cast_xa_q"), k_m, v_m,
                               batch=batch, seq=seq, mem=n_mem, tm=512, name=tag + "xa_attn")
        h, h_bf = _matmul_ln([(att, _cast_bf16(xa_w_o[layer], name=tag + "cast_xa_o"), 0)], h,
                             ln_g[layer, 1], ln_b[layer, 1], alpha=alpha, tm=512, tn=512, name=tag + "xa_out_ln1")

        act = _ffn_up(h_bf, ffn_w_gate[layer], ffn_w_up[layer], tm=2048, tn=256, name=tag + "ffn_up")
        ff = _matmul([(act, _cast_bf16(ffn_w_down[layer], name=tag + "cast_ffn_down"), 0)], F32, tm=512, tn=512,
                     name=tag + "ffn_down")
        h, h_bf = _residual_layer_norm(h, ff, ln_g[layer, 2], ln_b[layer, 2], alpha=alpha, tr=256,
                                       name=tag + "ln2")
    return h.reshape(batch, seq, d)
```

```python
import functools
import math

import jax
import jax.numpy as jnp
from jax import lax
from jax.experimental import pallas as pl
from jax.experimental.pallas import tpu as pltpu

F32 = jnp.float32
BF16 = jnp.bfloat16

LANES = 128
SUBLANES = 8
HEAD_DIM = LANES
VMEM_LIMIT_BYTES = 58 * 1024 * 1024

HGRN_CHUNK = 128
HGRN_LEVELS = 7
LOG2_E = 1.4426950408889634
WINDOW = 128
ATTN_BLOCK = 128
ATTN_GROUP = 4
ROPE_THETA = 500000.0
ROPE_DIM = HEAD_DIM // 4
GMLP_CHUNK = 128
MEM_HEADS = 4
LN_EPS = 1e-5
RMS_EPS = 1e-6
CAST_BLOCK_ELEMS = 1024 * 1024


def _params(*sem):
    return pltpu.CompilerParams(dimension_semantics=sem, vmem_limit_bytes=VMEM_LIMIT_BYTES)


def _pick(n, pref):
    t = min(pref, n)
    while n % t:
        t -= LANES
    return t


def _sigmoid(x):
    return 1.0 / (1.0 + jnp.exp(-x))


def _layer_norm_rows(y, g, b):
    mu = jnp.mean(y, axis=-1, keepdims=True)
    yc = y - mu
    var = jnp.mean(yc * yc, axis=-1, keepdims=True)
    return yc * lax.rsqrt(var + LN_EPS) * g + b


def _cast_body(x_ref, o_ref):
    o_ref[...] = x_ref[...].astype(o_ref.dtype)


def _cast_bf16(w, *, name):
    stack, lead = w
    _, r, c = stack.shape
    tr = _pick(r, max(LANES, (CAST_BLOCK_ELEMS // c) // LANES * LANES))
    return pl.pallas_call(
        _cast_body,
        out_shape=jax.ShapeDtypeStruct((r, c), BF16),
        grid=(r // tr,),
        in_specs=[pl.BlockSpec((None, tr, c), lambda i: (lead, i, 0))],
        out_specs=pl.BlockSpec((tr, c), lambda i: (i, 0)),
        compiler_params=_params("parallel"),
        name=name,
    )(stack)


def _w_spec(w, k, tn, imap):
    if isinstance(w, tuple):
        stack, lead = w
        return pl.BlockSpec((None, k, tn), lambda *ids: (lead,) + tuple(imap(*ids))), stack
    return pl.BlockSpec((k, tn), imap), w


def _w_cols(w):
    return (w[0] if isinstance(w, tuple) else w).shape[-1]


def _identity(x):
    return x


def _gelu_exact(x):
    return 0.5 * x * (1.0 + lax.erf(x * (1.0 / math.sqrt(2.0))))


def _dot_pairs(refs, n_pairs):
    acc = None
    for p in range(n_pairs):
        d = jnp.dot(refs[2 * p][...], refs[2 * p + 1][...].astype(BF16), preferred_element_type=F32)
        acc = d if acc is None else acc + d
    return acc


def _mm_body(*refs, n_pairs, epilogue):
    o_ref = refs[-1]
    o_ref[...] = epilogue(_dot_pairs(refs, n_pairs)).astype(o_ref.dtype)


def _pair_specs(pairs, tm, tn, a_once, row_tile=lambda i: i, col_tile=lambda i, j: j):
    in_specs, args = [], []
    a_mode = {"pipeline_mode": pl.Buffered(1)} if a_once else {}
    for a, w, kb in pairs:
        k = a.shape[1]
        w_spec, w_arg = _w_spec(w, k, tn, lambda i, j, kb=kb: (kb, col_tile(i, j)))
        in_specs += [pl.BlockSpec((tm, k), lambda i, j: (row_tile(i), 0), **a_mode), w_spec]
        args += [a, w_arg]
    return in_specs, args


def _matmul(pairs, out_dtype, *, tm, tn, epilogue=_identity, a_once=False, name):
    m = pairs[0][0].shape[0]
    n = _w_cols(pairs[0][1])
    tm = _pick(m, tm)
    tn = _pick(n, tn)
    in_specs, args = _pair_specs(pairs, tm, tn, a_once)
    return pl.pallas_call(
        functools.partial(_mm_body, n_pairs=len(pairs), epilogue=epilogue),
        out_shape=jax.ShapeDtypeStruct((m, n), out_dtype),
        grid=(m // tm, n // tn),
        in_specs=in_specs,
        out_specs=pl.BlockSpec((tm, tn), lambda i, j: (i, j)),
        compiler_params=_params("parallel", "arbitrary"),
        name=name,
    )(*args)


def _mm_ln_body(*refs, n_pairs, alpha, n_row, tn, inv_n):
    res_ref, g_ref, b_ref, of_ref, ob_ref, y_ref, mean_ref, m2_ref = refs[2 * n_pairs:]
    i = pl.program_id(0)
    j = pl.program_id(1)
    cur = i % 2
    prev = 1 - cur

    @pl.when((i == 0) & (j == 0))
    def _():
        y_ref[...] = jnp.zeros_like(y_ref)
        mean_ref[...] = jnp.zeros_like(mean_ref)
        m2_ref[...] = jnp.zeros_like(m2_ref)

    def normalise_previous():
        rstd = lax.rsqrt(m2_ref[prev] * inv_n + LN_EPS)
        o = (y_ref[j] - mean_ref[prev]) * rstd * g_ref[...] + b_ref[...]
        of_ref[...] = o
        ob_ref[...] = o.astype(BF16)

    @pl.when(i < n_row)
    def _():
        normalise_previous()
        y = alpha * res_ref[...] + _dot_pairs(refs, n_pairs)
        y_ref[j] = y
        m_j = jnp.mean(y, axis=-1, keepdims=True)
        d = y - m_j
        m2_j = jnp.sum(d * d, axis=-1, keepdims=True)
        n_a = (jnp.zeros((1, 1), jnp.int32) + j).astype(F32) * tn
        w_b = tn / (n_a + tn)
        keep = jnp.where(n_a > 0.0, 1.0, 0.0)
        delta = m_j - mean_ref[cur]
        mean_ref[cur] = mean_ref[cur] + delta * w_b
        m2_ref[cur] = keep * m2_ref[cur] + m2_j + delta * delta * (n_a * w_b)

    @pl.when(i == n_row)
    def _():
        normalise_previous()


def _matmul_ln(pairs, res, g, b, *, alpha, tm, tn, name):
    m, n = res.shape
    tm = _pick(m, tm)
    tn = _pick(n, tn)
    n_row, n_col = m // tm, n // tn

    def row_tile(i):
        return jnp.minimum(i, n_row - 1)

    def col_tile(i, j):
        return jnp.where(i < n_row, j, n_col - 1)

    in_specs, args = _pair_specs(pairs, tm, tn, False, row_tile, col_tile)
    vec = pl.BlockSpec((1, tn), lambda i, j: (0, j))
    out = pl.BlockSpec((tm, tn), lambda i, j: (jnp.maximum(i - 1, 0), jnp.where(i == 0, 0, j)))
    return pl.pallas_call(
        functools.partial(_mm_ln_body, n_pairs=len(pairs), alpha=alpha, n_row=n_row, tn=tn, inv_n=1.0 / n),
        out_shape=(jax.ShapeDtypeStruct((m, n), F32), jax.ShapeDtypeStruct((m, n), BF16)),
        grid=(n_row + 1, n_col),
        in_specs=in_specs + [pl.BlockSpec((tm, tn), lambda i, j: (row_tile(i), col_tile(i, j))), vec, vec],
        out_specs=(out, out),
        scratch_shapes=[pltpu.VMEM((n_col, tm, tn), F32), pltpu.VMEM((2, tm, 1), F32), pltpu.VMEM((2, tm, 1), F32)],
        compiler_params=_params("arbitrary", "arbitrary"),
        name=name,
    )(*args, res, g.reshape(1, n), b.reshape(1, n))


def _ffn_up_body(a_ref, wg_ref, wu_ref, o_ref):
    a = a_ref[...]
    g = jnp.dot(a, wg_ref[...].astype(BF16), preferred_element_type=F32)
    u = jnp.dot(a, wu_ref[...].astype(BF16), preferred_element_type=F32)
    o_ref[...] = (g * _sigmoid(g) * u).astype(o_ref.dtype)


def _ffn_up(a, wg, wu, *, tm, tn, name):
    m, k = a.shape
    n = _w_cols(wg)
    tm = _pick(m, tm)
    tn = _pick(n, tn)
    wg_spec, wg_arg = _w_spec(wg, k, tn, lambda i, j: (0, j))
    wu_spec, wu_arg = _w_spec(wu, k, tn, lambda i, j: (0, j))
    return pl.pallas_call(
        _ffn_up_body,
        out_shape=jax.ShapeDtypeStruct((m, n), BF16),
        grid=(m // tm, n // tn),
        in_specs=[pl.BlockSpec((tm, k), lambda i, j: (i, 0), pipeline_mode=pl.Buffered(1)), wg_spec, wu_spec],
        out_specs=pl.BlockSpec((tm, tn), lambda i, j: (i, j)),
        compiler_params=_params("parallel", "arbitrary"),
        name=name,
    )(a, wg_arg, wu_arg)


def _ln_body(h_ref, m_ref, g_ref, b_ref, of_ref, ob_ref, *, alpha):
    o = _layer_norm_rows(alpha * h_ref[...] + m_ref[...], g_ref[...], b_ref[...])
    of_ref[...] = o
    ob_ref[...] = o.astype(BF16)


def _residual_layer_norm(h, mix, g, b, *, alpha, tr, name):
    t, d = h.shape
    tr = _pick(t, tr)
    row = pl.BlockSpec((tr, d), lambda i: (i, 0))
    vec = pl.BlockSpec((1, d), lambda i: (0, 0))
    return pl.pallas_call(
        functools.partial(_ln_body, alpha=alpha),
        out_shape=(jax.ShapeDtypeStruct((t, d), F32), jax.ShapeDtypeStruct((t, d), BF16)),
        grid=(t // tr,),
        in_specs=[row, row, vec, vec],
        out_specs=(row, row),
        compiler_params=_params("parallel"),
        name=name,
    )(h, mix, g.reshape(1, d), b.reshape(1, d))


def _hgrn_chunk(z, q_raw, v, lb, st, lvl, sub, backward):
    c_len = z.shape[0]
    groups = c_len // SUBLANES
    shape3 = (groups, SUBLANES, LANES)
    e = jnp.exp(-jnp.abs(z))
    log_sig = jnp.minimum(z, 0.0) - jnp.log(1.0 + e)
    sig_neg = jnp.where(z >= 0.0, e, 1.0) / (1.0 + e)
    a = jnp.log(lb)
    c = jnp.log1p(-lb) + log_sig
    g = (jnp.maximum(a, c) + jnp.log(1.0 + jnp.exp(-jnp.abs(a - c)))) * LOG2_E
    k = ((1.0 - lb) * sig_neg).reshape(shape3)
    q = (q_raw * _sigmoid(q_raw)).reshape(shape3)
    g = g.reshape(shape3)

    loc = g
    for j in range(3):
        sh = 1 << j
        loc = loc + jnp.where(sub >= sh, pltpu.roll(loc, sh, 1), 0.0)
    locx = loc - g
    tot = jnp.broadcast_to(loc[:, SUBLANES - 1:SUBLANES, :], shape3)
    base = [jnp.zeros((SUBLANES, LANES), F32)]
    for r in range(groups):
        base.append(base[r] + tot[r])
    b = loc + jnp.stack(base[:groups])
    bx = b - g

    def bounds(m):
        span = 1 << (m - 3)
        first = [(r // span) * span for r in range(groups)]
        return (jnp.stack([base[f] for f in first]), jnp.stack([base[f + span] for f in first]))

    def factors(q_exp, k_exp):
        q_m = (q * jnp.exp2(q_exp)).reshape(c_len, LANES).astype(BF16)
        k_m = (k * jnp.exp2(k_exp)).reshape(c_len, LANES).astype(BF16)
        return q_m, k_m

    scores = jnp.zeros((c_len, c_len), F32)
    end_l, start_l = loc, locx
    for m in range(HGRN_LEVELS):
        if m < 3:
            q_exp, k_exp = (end_l - locx, locx - start_l) if backward else (loc - start_l, end_l - loc)
            sh = 1 << m
            upper = (sub & sh) != 0
            end_l = jnp.where(upper, end_l, pltpu.roll(end_l, SUBLANES - sh, 1))
            start_l = jnp.where(upper, pltpu.roll(start_l, sh, 1), start_l)
        else:
            start, end = bounds(m)
            q_exp, k_exp = (end - bx, bx - start) if backward else (b - start, end - b)
        q_m, k_m = factors(q_exp, k_exp)
        p_m = lax.dot_general(q_m, k_m, (((1,), (1,)), ((), ())), preferred_element_type=F32)
        scores = jnp.where(lvl == m + 1, p_m, scores)
    total = base[groups]
    q_c, k_c = factors(total - bx, bx) if backward else factors(b, total - b)
    decay = jnp.exp2(total[0:1, :])

    q2 = q.reshape(c_len, LANES)
    k2 = k.reshape(c_len, LANES)
    v_bf = v.astype(BF16)
    diag = jnp.sum(q2 * k2, axis=1, keepdims=True)
    o = (jnp.dot(scores.astype(BF16), v_bf, preferred_element_type=F32)
         + diag * v
         + lax.dot_general(q_c, st.astype(BF16), (((1,), (1,)), ((), ())), preferred_element_type=F32))
    st_new = st * decay + lax.dot_general(v_bf, k_c, (((0,), (0,)), ((), ())), preferred_element_type=F32)
    return o, st_new


def _hgrn_body(qf_ref, zf_ref, vf_ref, qb_ref, zb_ref, vb_ref, lb_ref, of_ref, ob_ref, st_ref, *, n_chunks, heads):
    @pl.when(pl.program_id(2) == 0)
    def _():
        st_ref[...] = jnp.zeros_like(st_ref)

    c_len = HGRN_CHUNK
    row = lax.broadcasted_iota(jnp.int32, (c_len, c_len), 0)
    col = lax.broadcasted_iota(jnp.int32, (c_len, c_len), 1)
    lvl = 32 - lax.clz(row ^ col)
    lvl_f = jnp.where(row > col, lvl, 0)
    lvl_b = jnp.where(row < col, lvl, 0)
    sub = lax.broadcasted_iota(jnp.int32, (c_len // SUBLANES, SUBLANES, LANES), 1)

    def step(c, carry):
        rf = pl.ds(pl.multiple_of(c * c_len, c_len), c_len)
        rb = pl.ds(pl.multiple_of((n_chunks - 1 - c) * c_len, c_len), c_len)
        for h in range(heads):
            sl = slice(h * HEAD_DIM, (h + 1) * HEAD_DIM)
            o_f, st_f = _hgrn_chunk(zf_ref[rf, sl], qf_ref[rf, sl], vf_ref[rf, sl], lb_ref[0, :, sl],
                                    st_ref[0, h], lvl_f, sub, False)
            of_ref[rf, sl] = o_f
            st_ref[0, h] = st_f
            o_b, st_b = _hgrn_chunk(zb_ref[rb, sl], qb_ref[rb, sl], vb_ref[rb, sl], lb_ref[1, :, sl],
                                    st_ref[1, h], lvl_b, sub, True)
            ob_ref[rb, sl] = o_b
            st_ref[1, h] = st_b
        return carry

    lax.fori_loop(0, n_chunks, step, 0)


def _hgrn_scan(proj, lb, *, batch, seq, width, tb, heads_per_step, name):
    hw = heads_per_step * HEAD_DIM
    head_groups = width // hw
    tb = _pick(seq, tb)
    nb = seq // tb
    n_chunks = tb // HGRN_CHUNK

    def spec(col0, reverse):
        def imap(b, h, t):
            tt = (nb - 1 - t) if reverse else t
            return (b * nb + tt, col0 * head_groups + h)
        return pl.BlockSpec((tb, hw), imap)

    out_f = pl.BlockSpec((tb, hw), lambda b, h, t: (b * nb + t, h))
    out_b = pl.BlockSpec((tb, hw), lambda b, h, t: (b * nb + (nb - 1 - t), h))
    t_rows = batch * seq
    return pl.pallas_call(
        functools.partial(_hgrn_body, n_chunks=n_chunks, heads=heads_per_step),
        out_shape=(jax.ShapeDtypeStruct((t_rows, width), F32), jax.ShapeDtypeStruct((t_rows, width), F32)),
        grid=(batch, head_groups, nb),
        in_specs=[spec(0, False), spec(1, False), spec(3, False),
                  spec(0, True), spec(2, True), spec(3, True),
                  pl.BlockSpec((2, 1, hw), lambda b, h, t: (0, 0, h))],
        out_specs=(out_f, out_b),
        scratch_shapes=[pltpu.VMEM((2, heads_per_step, HEAD_DIM, HEAD_DIM), F32)],
        compiler_params=_params("parallel", "parallel", "arbitrary"),
        name=name,
    )(proj, proj, proj, proj, proj, proj, lb.reshape(2, 1, width))


def _hgrn_out_body(of_ref, ob_ref, g_ref, ng_ref, o_ref, *, heads):
    for h in range(heads):
        sl = slice(h * HEAD_DIM, (h + 1) * HEAD_DIM)
        o = of_ref[:, sl] + ob_ref[:, sl]
        o = o * lax.rsqrt(jnp.mean(o * o, axis=-1, keepdims=True) + RMS_EPS)
        gate = g_ref[:, sl]
        o_ref[:, sl] = (o * ng_ref[:, sl] * (gate * _sigmoid(gate))).astype(o_ref.dtype)


def _hgrn_out(o_f, o_b, proj, norm_g, *, tr, name):
    t, width = o_f.shape
    tr = _pick(t, tr)
    row = pl.BlockSpec((tr, width), lambda i: (i, 0))
    return pl.pallas_call(
        functools.partial(_hgrn_out_body, heads=width // HEAD_DIM),
        out_shape=jax.ShapeDtypeStruct((t, width), BF16),
        grid=(t // tr,),
        in_specs=[row, row, pl.BlockSpec((tr, width), lambda i: (i, 4)),
                  pl.BlockSpec((1, width), lambda i: (0, 0))],
        out_specs=row,
        compiler_params=_params("parallel"),
        name=name,
    )(o_f, o_b, proj, norm_g.reshape(1, width))


def _rope_body(pos_ref, invf_ref, q_ref, k_ref, v_ref, qo_ref, ko_ref, vo_ref, *, q_heads, kv_heads, scale):
    half = ROPE_DIM // 2
    ang = pos_ref[...].astype(F32) * invf_ref[...]
    lane = lax.broadcasted_iota(jnp.int32, ang.shape, 1)
    cos = jnp.where(lane < ROPE_DIM, jnp.cos(ang), 1.0)
    sin = jnp.sin(ang)
    sin = jnp.where(lane < half, -sin, jnp.where(lane < ROPE_DIM, sin, 0.0))

    def rot(x):
        partner = jnp.where(lane < half, pltpu.roll(x, LANES - half, 1), pltpu.roll(x, half, 1))
        return x * cos + partner * sin

    for h in range(q_heads):
        sl = slice(h * HEAD_DIM, (h + 1) * HEAD_DIM)
        qo_ref[:, sl] = (rot(q_ref[:, sl]) * scale).astype(BF16)
    for h in range(kv_heads):
        sl = slice(h * HEAD_DIM, (h + 1) * HEAD_DIM)
        ko_ref[:, sl] = rot(k_ref[:, sl]).astype(BF16)
    vo_ref[...] = v_ref[...].astype(BF16)


def _rope(proj, positions, *, q_col, q_width, kv_width, tr, name):
    t = proj.shape[0]
    tr = _pick(t, tr)
    half = ROPE_DIM // 2
    inv_freq = jnp.power(jnp.float32(ROPE_THETA), -jnp.arange(half, dtype=F32) * (2.0 / ROPE_DIM))
    invf = jnp.zeros((1, LANES), F32).at[0, :ROPE_DIM].set(jnp.concatenate([inv_freq, inv_freq]))
    k_blk = (q_col + q_width) // kv_width
    return pl.pallas_call(
        functools.partial(_rope_body, q_heads=q_width // HEAD_DIM, kv_heads=kv_width // HEAD_DIM,
                          scale=HEAD_DIM ** -0.5),
        out_shape=(jax.ShapeDtypeStruct((t, q_width), BF16), jax.ShapeDtypeStruct((t, kv_width), BF16),
                   jax.ShapeDtypeStruct((t, kv_width), BF16)),
        grid=(t // tr,),
        in_specs=[pl.BlockSpec((tr, 1), lambda i: (i, 0)),
                  pl.BlockSpec((1, LANES), lambda i: (0, 0)),
                  pl.BlockSpec((tr, q_width), lambda i: (i, q_col // q_width)),
                  pl.BlockSpec((tr, kv_width), lambda i: (i, k_blk)),
                  pl.BlockSpec((tr, kv_width), lambda i: (i, k_blk + 1))],
        out_specs=(pl.BlockSpec((tr, q_width), lambda i: (i, 0)),
                   pl.BlockSpec((tr, kv_width), lambda i: (i, 0)),
                   pl.BlockSpec((tr, kv_width), lambda i: (i, 0))),
        compiler_params=_params("parallel"),
        name=name,
    )(positions.reshape(t, 1), invf, proj, proj, proj)


def _swa_body(sink_ref, q_ref, kp_ref, kc_ref, kn_ref, vp_ref, vc_ref, vn_ref, o_ref, *, n_blocks, kv_per_step):
    n = pl.program_id(2)
    blk = ATTN_BLOCK
    shape = (ATTN_GROUP * blk, 3 * blk)
    i = lax.broadcasted_iota(jnp.int32, shape, 0) & (blk - 1)
    j = lax.broadcasted_iota(jnp.int32, shape, 1)
    rel = j - i
    valid = (rel >= blk - WINDOW) & (rel <= blk + WINDOW)
    valid &= (j >= blk) | (n > 0)
    valid &= (j < 2 * blk) | (n < n_blocks - 1)
    r = lax.broadcasted_iota(jnp.int32, (ATTN_GROUP * blk, 1), 0)
    for hh in range(kv_per_step):
        h = pl.program_id(1) * kv_per_step + hh
        kv = slice(hh * HEAD_DIM, (hh + 1) * HEAD_DIM)
        q0 = hh * ATTN_GROUP * HEAD_DIM
        q = jnp.concatenate([q_ref[:, q0 + g * HEAD_DIM:q0 + (g + 1) * HEAD_DIM] for g in range(ATTN_GROUP)], axis=0)
        k = jnp.concatenate([kp_ref[:, kv], kc_ref[:, kv], kn_ref[:, kv]], axis=0)
        v = jnp.concatenate([vp_ref[:, kv], vc_ref[:, kv], vn_ref[:, kv]], axis=0)
        s = lax.dot_general(q, k, (((1,), (1,)), ((), ())), preferred_element_type=F32)
        s = jnp.where(valid, s, -jnp.inf)
        sink = jnp.zeros((ATTN_GROUP * blk, 1), F32)
        for g in range(ATTN_GROUP):
            sink = jnp.where(r >= g * blk, sink_ref[h * ATTN_GROUP + g], sink)
        m = jnp.maximum(jnp.max(s, axis=-1, keepdims=True), sink)
        p = jnp.exp(s - m)
        denom = jnp.sum(p, axis=-1, keepdims=True) + jnp.exp(sink - m)
        o = jnp.dot(p.astype(BF16), v, preferred_element_type=F32) / denom
        for g in range(ATTN_GROUP):
            o_ref[:, q0 + g * HEAD_DIM:q0 + (g + 1) * HEAD_DIM] = o[g * blk:(g + 1) * blk, :].astype(o_ref.dtype)


def _window_attention(q, k, v, sink, *, batch, seq, kv_per_step, name):
    t, q_width = q.shape
    kv_heads = k.shape[1] // HEAD_DIM
    kv_per_step = math.gcd(kv_per_step, kv_heads)
    nb = seq // ATTN_BLOCK
    gw = kv_per_step * ATTN_GROUP * HEAD_DIM

    def kv_spec(shift):
        def imap(b, h, n):
            return (b * nb + jnp.clip(n + shift, 0, nb - 1), h)
        return pl.BlockSpec((ATTN_BLOCK, kv_per_step * HEAD_DIM), imap)

    q_spec = pl.BlockSpec((ATTN_BLOCK, gw), lambda b, h, n: (b * nb + n, h))
    return pl.pallas_call(
        functools.partial(_swa_body, n_blocks=nb, kv_per_step=kv_per_step),
        out_shape=jax.ShapeDtypeStruct((t, q_width), BF16),
        grid=(batch, kv_heads // kv_per_step, nb),
        in_specs=[pl.BlockSpec(memory_space=pltpu.SMEM), q_spec,
                  kv_spec(-1), kv_spec(0), kv_spec(1), kv_spec(-1), kv_spec(0), kv_spec(1)],
        out_specs=q_spec,
        compiler_params=_params("parallel", "parallel", "arbitrary"),
        name=name,
    )(sink, q, k, k, k, v, v, v)


def _xattn_body(a_ref, wq_ref, k_ref, v_ref, o_ref, *, scale):
    q = jnp.dot(a_ref[...], wq_ref[...], preferred_element_type=F32) * scale
    s = lax.dot_general(q.astype(BF16), k_ref[...], (((1,), (1,)), ((), ())), preferred_element_type=F32)
    m = jnp.max(s, axis=-1, keepdims=True)
    p = jnp.exp(s - m)
    denom = jnp.sum(p, axis=-1, keepdims=True)
    o = jnp.dot(p.astype(BF16), v_ref[...], preferred_element_type=F32) / denom
    o_ref[...] = o.astype(o_ref.dtype)


def _cross_attention(a, wq, k, v, *, batch, seq, mem, tm, name):
    t, d = a.shape
    hd = d // MEM_HEADS
    tm = _pick(seq, tm)
    nb = seq // tm
    return pl.pallas_call(
        functools.partial(_xattn_body, scale=hd ** -0.5),
        out_shape=jax.ShapeDtypeStruct((t, d), BF16),
        grid=(batch, nb, MEM_HEADS),
        in_specs=[pl.BlockSpec((tm, d), lambda b, i, h: (b * nb + i, 0)),
                  pl.BlockSpec((d, hd), lambda b, i, h: (0, h)),
                  pl.BlockSpec((mem, hd), lambda b, i, h: (b, h)),
                  pl.BlockSpec((mem, hd), lambda b, i, h: (b, h))],
        out_specs=pl.BlockSpec((tm, hd), lambda b, i, h: (b * nb + i, h)),
        compiler_params=_params("parallel", "parallel", "arbitrary"),
        name=name,
    )(a, wq, k, v)


def _sgu_body(u_ref, v_ref, g_ref, b_ref, ws_ref, bs_ref, o_ref, *, groups):
    vn = _layer_norm_rows(v_ref[...], g_ref[...], b_ref[...]).astype(BF16)
    for g in range(groups):
        sl = slice(g * LANES, (g + 1) * LANES)
        s = jnp.dot(ws_ref[g], vn[:, sl], preferred_element_type=F32) + bs_ref[:, sl]
        o_ref[:, sl] = (u_ref[:, sl] * s).astype(o_ref.dtype)


def _spatial_gating(z, norm_g, norm_b, w_s, b_s, *, name):
    t, w2 = z.shape
    w = w2 // 2
    groups = w // LANES
    row_u = pl.BlockSpec((GMLP_CHUNK, w), lambda i: (i, 0))
    row_v = pl.BlockSpec((GMLP_CHUNK, w), lambda i: (i, 1))
    vec = pl.BlockSpec((1, w), lambda i: (0, 0))
    bs = jnp.repeat(b_s.T, LANES, axis=1)
    return pl.pallas_call(
        functools.partial(_sgu_body, groups=groups),
        out_shape=jax.ShapeDtypeStruct((t, w), BF16),
        grid=(t // GMLP_CHUNK,),
        in_specs=[row_u, row_v, vec, vec,
                  pl.BlockSpec((groups, GMLP_CHUNK, GMLP_CHUNK), lambda i: (0, 0, 0)),
                  pl.BlockSpec((GMLP_CHUNK, w), lambda i: (0, 0))],
        out_specs=row_u,
        compiler_params=_params("parallel"),
        name=name,
    )(z, z, norm_g.reshape(1, w), norm_b.reshape(1, w), w_s.astype(BF16), bs)


def kernel(x, mem, positions, ab_w_in, hgrn_lower_bounds, hgrn_norm_g, attn_sink, ab_w_out,
           gmlp_w_in, gmlp_norm_g, gmlp_norm_b, gmlp_w_spatial, gmlp_b_spatial, gmlp_w_out,
           xa_w_q, xa_w_k, xa_w_v, xa_w_o, ffn_w_gate, ffn_w_up, ffn_w_down, ln_g, ln_b):
    batch, seq, d = x.shape
    n_mem = mem.shape[1]
    depth = ln_g.shape[0]
    t = batch * seq
    alpha = (2.0 * depth) ** 0.25
    hgrn_w = d // 2
    attn_w = d - hgrn_w
    kv_w = attn_w // ATTN_GROUP

    lb_all = jnp.cumsum(jax.nn.softmax(hgrn_lower_bounds.astype(F32), axis=1), axis=1)

    h = x.reshape(t, d)
    h_bf = h.astype(BF16)
    mem_bf = mem.reshape(batch * n_mem, d).astype(BF16)

    for layer in range(depth):
        j = layer // 2
        tag = "l%d_" % layer
        if layer % 2 == 0:
            proj = _matmul([(h_bf, (ab_w_in, j), 0)], F32, tm=2048, tn=512, a_once=True, name=tag + "mix_in")
            o_f, o_b = _hgrn_scan(proj, lb_all[:, layer], batch=batch, seq=seq, width=hgrn_w, tb=512,
                                  heads_per_step=2, name=tag + "hgrn_scan")
            a_out = _hgrn_out(o_f, o_b, proj, hgrn_norm_g[j], tr=256, name=tag + "hgrn_out")
            q_r, k_r, v_r = _rope(proj, positions, q_col=5 * hgrn_w, q_width=attn_w, kv_width=kv_w, tr=256,
                                  name=tag + "rope")
            b_out = _window_attention(q_r, k_r, v_r, attn_sink[j].astype(F32), batch=batch, seq=seq,
                                      kv_per_step=2, name=tag + "swa")
            w_out = _cast_bf16((ab_w_out, j), name=tag + "cast_mix_out")
            mix_pairs = [(a_out, w_out, 0), (b_out, w_out, 1)]
        else:
            z = _matmul([(h_bf, (gmlp_w_in, j), 0)], F32, tm=2048, tn=512, epilogue=_gelu_exact, a_once=True,
                        name=tag + "gmlp_in")
            gated = _spatial_gating(z, gmlp_norm_g[j], gmlp_norm_b[j], gmlp_w_spatial[j], gmlp_b_spatial[j],
                                    name=tag + "gmlp_sgu")
            mix_pairs = [(gated, _cast_bf16((gmlp_w_out, j), name=tag + "cast_mix_out"), 0)]
        h, h_bf = _matmul_ln(mix_pairs, h, ln_g[layer, 0], ln_b[layer, 0], alpha=alpha, tm=1024, tn=512,
                             name=tag + "mix_out_ln0")

        k_m = _matmul([(mem_bf, (xa_w_k, layer), 0)], BF16, tm=512, tn=512, name=tag + "xa_k")
        v_m = _matmul([(mem_bf, (xa_w_v, layer), 0)], BF16, tm=512, tn=512, name=tag + "xa_v")
        att = _cross_attention(h_bf, _cast_bf16((xa_w_q, layer), name=tag + "cast_xa_q"), k_m, v_m,
                               batch=batch, seq=seq, mem=n_mem, tm=512, name=tag + "xa_attn")
        h, h_bf = _matmul_ln([(att, _cast_bf16((xa_w_o, layer), name=tag + "cast_xa_o"), 0)], h,
                             ln_g[layer, 1], ln_b[layer, 1], alpha=alpha, tm=1024, tn=512, name=tag + "xa_out_ln1")

        act = _ffn_up(h_bf, (ffn_w_gate, layer), (ffn_w_up, layer), tm=2048, tn=256, name=tag + "ffn_up")
        ff = _matmul([(act, _cast_bf16((ffn_w_down, layer), name=tag + "cast_ffn_down"), 0)], F32, tm=512, tn=512,
                     name=tag + "ffn_down")
        h, h_bf = _residual_layer_norm(h, ff, ln_g[layer, 2], ln_b[layer, 2], alpha=alpha, tr=256,
                                       name=tag + "ln2")
    return h.reshape(batch, seq, d)
```

```python
import functools
import math

import jax
import jax.numpy as jnp
from jax import lax
from jax.experimental import pallas as pl
from jax.experimental.pallas import tpu as pltpu

F32 = jnp.float32
BF16 = jnp.bfloat16

LANES = 128
SUBLANES = 8
BF16_ROWS = 16
HEAD_DIM = LANES
VMEM_LIMIT_BYTES = 58 * 1024 * 1024

HGRN_CHUNK = 128
HGRN_LEVELS = 7
LOG2_E = 1.4426950408889634
WINDOW = 128
ATTN_BLOCK = 128
ATTN_GROUP = 4
ROPE_THETA = 500000.0
ROPE_DIM = HEAD_DIM // 4
GMLP_CHUNK = 128
MEM_HEADS = 4
LN_EPS = 1e-5
RMS_EPS = 1e-6


def _params(*sem):
    return pltpu.CompilerParams(dimension_semantics=sem, vmem_limit_bytes=VMEM_LIMIT_BYTES)


def _pick(n, pref):
    t = min(pref, n)
    while n % t:
        t -= LANES
    return t


def _sigmoid(x):
    return 1.0 / (1.0 + jnp.exp(-x))


def _layer_norm_rows(y, g, b):
    mu = jnp.mean(y, axis=-1, keepdims=True)
    yc = y - mu
    var = jnp.mean(yc * yc, axis=-1, keepdims=True)
    return yc * lax.rsqrt(var + LN_EPS) * g + b


def _w_spec(w, k, tn, imap):
    if isinstance(w, tuple):
        stack, lead = w
        return pl.BlockSpec((None, k, tn), lambda *ids: (lead,) + tuple(imap(*ids))), stack
    return pl.BlockSpec((k, tn), imap), w


def _w_cols(w):
    return (w[0] if isinstance(w, tuple) else w).shape[-1]


def _identity(x):
    return x


def _gelu_exact(x):
    return 0.5 * x * (1.0 + lax.erf(x * (1.0 / math.sqrt(2.0))))


def _dot_pairs(refs, n_pairs):
    acc = None
    for p in range(n_pairs):
        d = jnp.dot(refs[2 * p][...], refs[2 * p + 1][...].astype(BF16), preferred_element_type=F32)
        acc = d if acc is None else acc + d
    return acc


def _side_cast_specs(casts, grid):
    n_steps = grid[0] * grid[1]
    in_specs, args, out_specs, out_shapes = [], [], [], []
    for stack, lead in casts:
        _, r, c = stack.shape
        rows = BF16_ROWS
        while r % rows or r // rows > n_steps:
            rows += BF16_ROWS
        last = r // rows - 1

        def slab(i, j, last=last):
            return jnp.minimum(i * grid[1] + j, last)
        in_specs.append(pl.BlockSpec((None, rows, c), lambda i, j, lead=lead, slab=slab: (lead, slab(i, j), 0)))
        out_specs.append(pl.BlockSpec((rows, c), lambda i, j, slab=slab: (slab(i, j), 0)))
        out_shapes.append(jax.ShapeDtypeStruct((r, c), BF16))
        args.append(stack)
    return in_specs, args, out_specs, out_shapes


def _run_side_casts(side_in, side_out):
    for x_ref, c_ref in zip(side_in, side_out):
        c_ref[...] = x_ref[...].astype(BF16)


def _mm_body(*refs, n_pairs, n_side, epilogue):
    n_in = 2 * n_pairs + n_side
    o_ref = refs[n_in]
    o_ref[...] = epilogue(_dot_pairs(refs, n_pairs)).astype(o_ref.dtype)
    _run_side_casts(refs[2 * n_pairs:n_in], refs[n_in + 1:])


def _pair_specs(pairs, tm, tn, a_once, row_tile=lambda i: i, col_tile=lambda i, j: j):
    in_specs, args = [], []
    a_mode = {"pipeline_mode": pl.Buffered(1)} if a_once else {}
    for a, w, kb in pairs:
        k = a.shape[1]
        w_spec, w_arg = _w_spec(w, k, tn, lambda i, j, kb=kb: (kb, col_tile(i, j)))
        in_specs += [pl.BlockSpec((tm, k), lambda i, j: (row_tile(i), 0), **a_mode), w_spec]
        args += [a, w_arg]
    return in_specs, args


def _matmul(pairs, out_dtype, *, tm, tn, epilogue=_identity, a_once=False, side_casts=(), name):
    m = pairs[0][0].shape[0]
    n = _w_cols(pairs[0][1])
    tm = _pick(m, tm)
    tn = _pick(n, tn)
    grid = (m // tm, n // tn)
    in_specs, args = _pair_specs(pairs, tm, tn, a_once)
    c_in, c_args, c_out, c_shapes = _side_cast_specs(side_casts, grid)
    outs = pl.pallas_call(
        functools.partial(_mm_body, n_pairs=len(pairs), n_side=len(c_args), epilogue=epilogue),
        out_shape=[jax.ShapeDtypeStruct((m, n), out_dtype)] + c_shapes,
        grid=grid,
        in_specs=in_specs + c_in,
        out_specs=[pl.BlockSpec((tm, tn), lambda i, j: (i, j))] + c_out,
        compiler_params=_params("arbitrary", "arbitrary"),
        name=name,
    )(*args, *c_args)
    return outs if side_casts else outs[0]


def _mm_ln_body(*refs, n_pairs, alpha, n_row, tn, inv_n):
    res_ref, g_ref, b_ref, of_ref, ob_ref, y_ref, mean_ref, m2_ref = refs[2 * n_pairs:]
    i = pl.program_id(0)
    j = pl.program_id(1)
    cur = i % 2
    prev = 1 - cur

    @pl.when((i == 0) & (j == 0))
    def _():
        y_ref[...] = jnp.zeros_like(y_ref)
        mean_ref[...] = jnp.zeros_like(mean_ref)
        m2_ref[...] = jnp.zeros_like(m2_ref)

    def normalise_previous():
        rstd = lax.rsqrt(m2_ref[prev] * inv_n + LN_EPS)
        o = (y_ref[j] - mean_ref[prev]) * rstd * g_ref[...] + b_ref[...]
        of_ref[...] = o
        ob_ref[...] = o.astype(BF16)

    @pl.when(i < n_row)
    def _():
        normalise_previous()
        y = alpha * res_ref[...] + _dot_pairs(refs, n_pairs)
        y_ref[j] = y
        m_j = jnp.mean(y, axis=-1, keepdims=True)
        d = y - m_j
        m2_j = jnp.sum(d * d, axis=-1, keepdims=True)
        n_a = (jnp.zeros((1, 1), jnp.int32) + j).astype(F32) * tn
        w_b = tn / (n_a + tn)
        keep = jnp.where(n_a > 0.0, 1.0, 0.0)
        delta = m_j - mean_ref[cur]
        mean_ref[cur] = mean_ref[cur] + delta * w_b
        m2_ref[cur] = keep * m2_ref[cur] + m2_j + delta * delta * (n_a * w_b)

    @pl.when(i == n_row)
    def _():
        normalise_previous()


def _matmul_ln(pairs, res, g, b, *, alpha, tm, tn, name):
    m, n = res.shape
    tm = _pick(m, tm)
    tn = _pick(n, tn)
    n_row, n_col = m // tm, n // tn

    def row_tile(i):
        return jnp.minimum(i, n_row - 1)

    def col_tile(i, j):
        return jnp.where(i < n_row, j, n_col - 1)

    in_specs, args = _pair_specs(pairs, tm, tn, False, row_tile, col_tile)
    vec = pl.BlockSpec((1, tn), lambda i, j: (0, j))
    out = pl.BlockSpec((tm, tn), lambda i, j: (jnp.maximum(i - 1, 0), jnp.where(i == 0, 0, j)))
    return pl.pallas_call(
        functools.partial(_mm_ln_body, n_pairs=len(pairs), alpha=alpha, n_row=n_row, tn=tn, inv_n=1.0 / n),
        out_shape=(jax.ShapeDtypeStruct((m, n), F32), jax.ShapeDtypeStruct((m, n), BF16)),
        grid=(n_row + 1, n_col),
        in_specs=in_specs + [pl.BlockSpec((tm, tn), lambda i, j: (row_tile(i), col_tile(i, j))), vec, vec],
        out_specs=(out, out),
        scratch_shapes=[pltpu.VMEM((n_col, tm, tn), F32), pltpu.VMEM((2, tm, 1), F32), pltpu.VMEM((2, tm, 1), F32)],
        compiler_params=_params("arbitrary", "arbitrary"),
        name=name,
    )(*args, res, g.reshape(1, n), b.reshape(1, n))


def _ffn_up_body(a_ref, wg_ref, wu_ref, *refs, n_side):
    a = a_ref[...]
    g = jnp.dot(a, wg_ref[...].astype(BF16), preferred_element_type=F32)
    u = jnp.dot(a, wu_ref[...].astype(BF16), preferred_element_type=F32)
    o_ref = refs[n_side]
    o_ref[...] = (g * _sigmoid(g) * u).astype(o_ref.dtype)
    _run_side_casts(refs[:n_side], refs[n_side + 1:])


def _ffn_up(a, wg, wu, *, tm, tn, side_casts=(), name):
    m, k = a.shape
    n = _w_cols(wg)
    tm = _pick(m, tm)
    tn = _pick(n, tn)
    grid = (m // tm, n // tn)
    wg_spec, wg_arg = _w_spec(wg, k, tn, lambda i, j: (0, j))
    wu_spec, wu_arg = _w_spec(wu, k, tn, lambda i, j: (0, j))
    c_in, c_args, c_out, c_shapes = _side_cast_specs(side_casts, grid)
    outs = pl.pallas_call(
        functools.partial(_ffn_up_body, n_side=len(c_args)),
        out_shape=[jax.ShapeDtypeStruct((m, n), BF16)] + c_shapes,
        grid=grid,
        in_specs=[pl.BlockSpec((tm, k), lambda i, j: (i, 0), pipeline_mode=pl.Buffered(1)), wg_spec, wu_spec] + c_in,
        out_specs=[pl.BlockSpec((tm, tn), lambda i, j: (i, j))] + c_out,
        compiler_params=_params("arbitrary", "arbitrary"),
        name=name,
    )(a, wg_arg, wu_arg, *c_args)
    return outs if side_casts else outs[0]


def _hgrn_chunk(z, q_raw, v, lb, st, lvl, sub, backward):
    c_len = z.shape[0]
    groups = c_len // SUBLANES
    shape3 = (groups, SUBLANES, LANES)
    e = jnp.exp(-jnp.abs(z))
    log_sig = jnp.minimum(z, 0.0) - jnp.log(1.0 + e)
    sig_neg = jnp.where(z >= 0.0, e, 1.0) / (1.0 + e)
    a = jnp.log(lb)
    c = jnp.log1p(-lb) + log_sig
    g = (jnp.maximum(a, c) + jnp.log(1.0 + jnp.exp(-jnp.abs(a - c)))) * LOG2_E
    k = ((1.0 - lb) * sig_neg).reshape(shape3)
    q = (q_raw * _sigmoid(q_raw)).reshape(shape3)
    g = g.reshape(shape3)

    loc = g
    for j in range(3):
        sh = 1 << j
        loc = loc + jnp.where(sub >= sh, pltpu.roll(loc, sh, 1), 0.0)
    locx = loc - g
    tot = jnp.broadcast_to(loc[:, SUBLANES - 1:SUBLANES, :], shape3)
    base = [jnp.zeros((SUBLANES, LANES), F32)]
    for r in range(groups):
        base.append(base[r] + tot[r])
    b = loc + jnp.stack(base[:groups])
    bx = b - g

    def bounds(m):
        span = 1 << (m - 3)
        first = [(r // span) * span for r in range(groups)]
        return (jnp.stack([base[f] for f in first]), jnp.stack([base[f + span] for f in first]))

    def factors(q_exp, k_exp):
        q_m = (q * jnp.exp2(q_exp)).reshape(c_len, LANES).astype(BF16)
        k_m = (k * jnp.exp2(k_exp)).reshape(c_len, LANES).astype(BF16)
        return q_m, k_m

    scores = jnp.zeros((c_len, c_len), F32)
    end_l, start_l = loc, locx
    for m in range(HGRN_LEVELS):
        if m < 3:
            q_exp, k_exp = (end_l - locx, locx - start_l) if backward else (loc - start_l, end_l - loc)
            sh = 1 << m
            upper = (sub & sh) != 0
            end_l = jnp.where(upper, end_l, pltpu.roll(end_l, SUBLANES - sh, 1))
            start_l = jnp.where(upper, pltpu.roll(start_l, sh, 1), start_l)
        else:
            start, end = bounds(m)
            q_exp, k_exp = (end - bx, bx - start) if backward else (b - start, end - b)
        q_m, k_m = factors(q_exp, k_exp)
        p_m = lax.dot_general(q_m, k_m, (((1,), (1,)), ((), ())), preferred_element_type=F32)
        scores = jnp.where(lvl == m + 1, p_m, scores)
    total = base[groups]
    q_c, k_c = factors(total - bx, bx) if backward else factors(b, total - b)
    decay = jnp.exp2(total[0:1, :])

    q2 = q.reshape(c_len, LANES)
    k2 = k.reshape(c_len, LANES)
    v_bf = v.astype(BF16)
    diag = jnp.sum(q2 * k2, axis=1, keepdims=True)
    o = (jnp.dot(scores.astype(BF16), v_bf, preferred_element_type=F32)
         + diag * v
         + lax.dot_general(q_c, st.astype(BF16), (((1,), (1,)), ((), ())), preferred_element_type=F32))
    st_new = st * decay + lax.dot_general(v_bf, k_c, (((0,), (0,)), ((), ())), preferred_element_type=F32)
    return o, st_new


def _hgrn_body(qf_ref, zf_ref, vf_ref, qb_ref, zb_ref, vb_ref, lb_ref, of_ref, ob_ref, st_ref, *, n_chunks, heads):
    @pl.when(pl.program_id(2) == 0)
    def _():
        st_ref[...] = jnp.zeros_like(st_ref)

    c_len = HGRN_CHUNK
    row = lax.broadcasted_iota(jnp.int32, (c_len, c_len), 0)
    col = lax.broadcasted_iota(jnp.int32, (c_len, c_len), 1)
    lvl = 32 - lax.clz(row ^ col)
    lvl_f = jnp.where(row > col, lvl, 0)
    lvl_b = jnp.where(row < col, lvl, 0)
    sub = lax.broadcasted_iota(jnp.int32, (c_len // SUBLANES, SUBLANES, LANES), 1)

    def step(c, carry):
        rf = pl.ds(pl.multiple_of(c * c_len, c_len), c_len)
        rb = pl.ds(pl.multiple_of((n_chunks - 1 - c) * c_len, c_len), c_len)
        for h in range(heads):
            sl = slice(h * HEAD_DIM, (h + 1) * HEAD_DIM)
            o_f, st_f = _hgrn_chunk(zf_ref[rf, sl], qf_ref[rf, sl], vf_ref[rf, sl], lb_ref[0, :, sl],
                                    st_ref[0, h], lvl_f, sub, False)
            of_ref[rf, sl] = o_f
            st_ref[0, h] = st_f
            o_b, st_b = _hgrn_chunk(zb_ref[rb, sl], qb_ref[rb, sl], vb_ref[rb, sl], lb_ref[1, :, sl],
                                    st_ref[1, h], lvl_b, sub, True)
            ob_ref[rb, sl] = o_b
            st_ref[1, h] = st_b
        return carry

    lax.fori_loop(0, n_chunks, step, 0)


def _hgrn_scan(proj, lb, *, batch, seq, width, tb, heads_per_step, name):
    hw = heads_per_step * HEAD_DIM
    head_groups = width // hw
    tb = _pick(seq, tb)
    nb = seq // tb
    n_chunks = tb // HGRN_CHUNK

    def spec(col0, reverse):
        def imap(b, h, t):
            tt = (nb - 1 - t) if reverse else t
            return (b * nb + tt, col0 * head_groups + h)
        return pl.BlockSpec((tb, hw), imap)

    out_f = pl.BlockSpec((tb, hw), lambda b, h, t: (b * nb + t, h))
    out_b = pl.BlockSpec((tb, hw), lambda b, h, t: (b * nb + (nb - 1 - t), h))
    t_rows = batch * seq
    return pl.pallas_call(
        functools.partial(_hgrn_body, n_chunks=n_chunks, heads=heads_per_step),
        out_shape=(jax.ShapeDtypeStruct((t_rows, width), F32), jax.ShapeDtypeStruct((t_rows, width), F32)),
        grid=(batch, head_groups, nb),
        in_specs=[spec(0, False), spec(1, False), spec(3, False),
                  spec(0, True), spec(2, True), spec(3, True),
                  pl.BlockSpec((2, 1, hw), lambda b, h, t: (0, 0, h))],
        out_specs=(out_f, out_b),
        scratch_shapes=[pltpu.VMEM((2, heads_per_step, HEAD_DIM, HEAD_DIM), F32)],
        compiler_params=_params("parallel", "parallel", "arbitrary"),
        name=name,
    )(proj, proj, proj, proj, proj, proj, lb.reshape(2, 1, width))


def _hgrn_out_body(of_ref, ob_ref, g_ref, ng_ref, o_ref, *, heads):
    for h in range(heads):
        sl = slice(h * HEAD_DIM, (h + 1) * HEAD_DIM)
        o = of_ref[:, sl] + ob_ref[:, sl]
        o = o * lax.rsqrt(jnp.mean(o * o, axis=-1, keepdims=True) + RMS_EPS)
        gate = g_ref[:, sl]
        o_ref[:, sl] = (o * ng_ref[:, sl] * (gate * _sigmoid(gate))).astype(o_ref.dtype)


def _hgrn_out(o_f, o_b, proj, norm_g, *, tr, name):
    t, width = o_f.shape
    tr = _pick(t, tr)
    row = pl.BlockSpec((tr, width), lambda i: (i, 0))
    return pl.pallas_call(
        functools.partial(_hgrn_out_body, heads=width // HEAD_DIM),
        out_shape=jax.ShapeDtypeStruct((t, width), BF16),
        grid=(t // tr,),
        in_specs=[row, row, pl.BlockSpec((tr, width), lambda i: (i, 4)),
                  pl.BlockSpec((1, width), lambda i: (0, 0))],
        out_specs=row,
        compiler_params=_params("parallel"),
        name=name,
    )(o_f, o_b, proj, norm_g.reshape(1, width))


def _rope_body(pos_ref, invf_ref, q_ref, k_ref, v_ref, qo_ref, ko_ref, vo_ref, *, q_heads, kv_heads, scale):
    half = ROPE_DIM // 2
    ang = pos_ref[...].astype(F32) * invf_ref[...]
    lane = lax.broadcasted_iota(jnp.int32, ang.shape, 1)
    cos = jnp.where(lane < ROPE_DIM, jnp.cos(ang), 1.0)
    sin = jnp.sin(ang)
    sin = jnp.where(lane < half, -sin, jnp.where(lane < ROPE_DIM, sin, 0.0))

    def rot(x):
        partner = jnp.where(lane < half, pltpu.roll(x, LANES - half, 1), pltpu.roll(x, half, 1))
        return x * cos + partner * sin

    for h in range(q_heads):
        sl = slice(h * HEAD_DIM, (h + 1) * HEAD_DIM)
        qo_ref[:, sl] = (rot(q_ref[:, sl]) * scale).astype(BF16)
    for h in range(kv_heads):
        sl = slice(h * HEAD_DIM, (h + 1) * HEAD_DIM)
        ko_ref[:, sl] = rot(k_ref[:, sl]).astype(BF16)
    vo_ref[...] = v_ref[...].astype(BF16)


def _rope(proj, positions, *, q_col, q_width, kv_width, tr, name):
    t = proj.shape[0]
    tr = _pick(t, tr)
    half = ROPE_DIM // 2
    inv_freq = jnp.power(jnp.float32(ROPE_THETA), -jnp.arange(half, dtype=F32) * (2.0 / ROPE_DIM))
    invf = jnp.zeros((1, LANES), F32).at[0, :ROPE_DIM].set(jnp.concatenate([inv_freq, inv_freq]))
    k_blk = (q_col + q_width) // kv_width
    return pl.pallas_call(
        functools.partial(_rope_body, q_heads=q_width // HEAD_DIM, kv_heads=kv_width // HEAD_DIM,
                          scale=HEAD_DIM ** -0.5),
        out_shape=(jax.ShapeDtypeStruct((t, q_width), BF16), jax.ShapeDtypeStruct((t, kv_width), BF16),
                   jax.ShapeDtypeStruct((t, kv_width), BF16)),
        grid=(t // tr,),
        in_specs=[pl.BlockSpec((tr, 1), lambda i: (i, 0)),
                  pl.BlockSpec((1, LANES), lambda i: (0, 0)),
                  pl.BlockSpec((tr, q_width), lambda i: (i, q_col // q_width)),
                  pl.BlockSpec((tr, kv_width), lambda i: (i, k_blk)),
                  pl.BlockSpec((tr, kv_width), lambda i: (i, k_blk + 1))],
        out_specs=(pl.BlockSpec((tr, q_width), lambda i: (i, 0)),
                   pl.BlockSpec((tr, kv_width), lambda i: (i, 0)),
                   pl.BlockSpec((tr, kv_width), lambda i: (i, 0))),
        compiler_params=_params("parallel"),
        name=name,
    )(positions.reshape(t, 1), invf, proj, proj, proj)


def _swa_body(sink_ref, q_ref, kp_ref, kc_ref, kn_ref, vp_ref, vc_ref, vn_ref, o_ref, *, n_blocks, kv_per_step):
    n = pl.program_id(2)
    blk = ATTN_BLOCK
    shape = (ATTN_GROUP * blk, 3 * blk)
    i = lax.broadcasted_iota(jnp.int32, shape, 0) & (blk - 1)
    j = lax.broadcasted_iota(jnp.int32, shape, 1)
    rel = j - i
    valid = (rel >= blk - WINDOW) & (rel <= blk + WINDOW)
    valid &= (j >= blk) | (n > 0)
    valid &= (j < 2 * blk) | (n < n_blocks - 1)
    r = lax.broadcasted_iota(jnp.int32, (ATTN_GROUP * blk, 1), 0)
    for hh in range(kv_per_step):
        h = pl.program_id(1) * kv_per_step + hh
        kv = slice(hh * HEAD_DIM, (hh + 1) * HEAD_DIM)
        q0 = hh * ATTN_GROUP * HEAD_DIM
        q = jnp.concatenate([q_ref[:, q0 + g * HEAD_DIM:q0 + (g + 1) * HEAD_DIM] for g in range(ATTN_GROUP)], axis=0)
        k = jnp.concatenate([kp_ref[:, kv], kc_ref[:, kv], kn_ref[:, kv]], axis=0)
        v = jnp.concatenate([vp_ref[:, kv], vc_ref[:, kv], vn_ref[:, kv]], axis=0)
        s = lax.dot_general(q, k, (((1,), (1,)), ((), ())), preferred_element_type=F32)
        s = jnp.where(valid, s, -jnp.inf)
        sink = jnp.zeros((ATTN_GROUP * blk, 1), F32)
        for g in range(ATTN_GROUP):
            sink = jnp.where(r >= g * blk, sink_ref[h * ATTN_GROUP + g], sink)
        m = jnp.maximum(jnp.max(s, axis=-1, keepdims=True), sink)
        p = jnp.exp(s - m)
        denom = jnp.sum(p, axis=-1, keepdims=True) + jnp.exp(sink - m)
        o = jnp.dot(p.astype(BF16), v, preferred_element_type=F32) / denom
        for g in range(ATTN_GROUP):
            o_ref[:, q0 + g * HEAD_DIM:q0 + (g + 1) * HEAD_DIM] = o[g * blk:(g + 1) * blk, :].astype(o_ref.dtype)


def _window_attention(q, k, v, sink, *, batch, seq, kv_per_step, name):
    t, q_width = q.shape
    kv_heads = k.shape[1] // HEAD_DIM
    kv_per_step = math.gcd(kv_per_step, kv_heads)
    nb = seq // ATTN_BLOCK
    gw = kv_per_step * ATTN_GROUP * HEAD_DIM

    def kv_spec(shift):
        def imap(b, h, n):
            return (b * nb + jnp.clip(n + shift, 0, nb - 1), h)
        return pl.BlockSpec((ATTN_BLOCK, kv_per_step * HEAD_DIM), imap)

    q_spec = pl.BlockSpec((ATTN_BLOCK, gw), lambda b, h, n: (b * nb + n, h))
    return pl.pallas_call(
        functools.partial(_swa_body, n_blocks=nb, kv_per_step=kv_per_step),
        out_shape=jax.ShapeDtypeStruct((t, q_width), BF16),
        grid=(batch, kv_heads // kv_per_step, nb),
        in_specs=[pl.BlockSpec(memory_space=pltpu.SMEM), q_spec,
                  kv_spec(-1), kv_spec(0), kv_spec(1), kv_spec(-1), kv_spec(0), kv_spec(1)],
        out_specs=q_spec,
        compiler_params=_params("parallel", "parallel", "arbitrary"),
        name=name,
    )(sink, q, k, k, k, v, v, v)


def _xattn_body(a_ref, wq_ref, k_ref, v_ref, o_ref, *, scale):
    q = jnp.dot(a_ref[...], wq_ref[...], preferred_element_type=F32) * scale
    s = lax.dot_general(q.astype(BF16), k_ref[...], (((1,), (1,)), ((), ())), preferred_element_type=F32)
    m = jnp.max(s, axis=-1, keepdims=True)
    p = jnp.exp(s - m)
    denom = jnp.sum(p, axis=-1, keepdims=True)
    o = jnp.dot(p.astype(BF16), v_ref[...], preferred_element_type=F32) / denom
    o_ref[...] = o.astype(o_ref.dtype)


def _cross_attention(a, wq, k, v, *, batch, seq, mem, tm, name):
    t, d = a.shape
    hd = d // MEM_HEADS
    tm = _pick(seq, tm)
    nb = seq // tm
    return pl.pallas_call(
        functools.partial(_xattn_body, scale=hd ** -0.5),
        out_shape=jax.ShapeDtypeStruct((t, d), BF16),
        grid=(batch, nb, MEM_HEADS),
        in_specs=[pl.BlockSpec((tm, d), lambda b, i, h: (b * nb + i, 0)),
                  pl.BlockSpec((d, hd), lambda b, i, h: (0, h)),
                  pl.BlockSpec((mem, hd), lambda b, i, h: (b, h)),
                  pl.BlockSpec((mem, hd), lambda b, i, h: (b, h))],
        out_specs=pl.BlockSpec((tm, hd), lambda b, i, h: (b * nb + i, h)),
        compiler_params=_params("parallel", "parallel", "arbitrary"),
        name=name,
    )(a, wq, k, v)


def _sgu_body(u_ref, v_ref, g_ref, b_ref, ws_ref, bs_ref, o_ref, *, groups):
    vn = _layer_norm_rows(v_ref[...], g_ref[...], b_ref[...]).astype(BF16)
    for g in range(groups):
        sl = slice(g * LANES, (g + 1) * LANES)
        s = jnp.dot(ws_ref[g], vn[:, sl], preferred_element_type=F32) + bs_ref[:, sl]
        o_ref[:, sl] = (u_ref[:, sl] * s).astype(o_ref.dtype)


def _spatial_gating(z, norm_g, norm_b, w_s, b_s, *, name):
    t, w2 = z.shape
    w = w2 // 2
    groups = w // LANES
    row_u = pl.BlockSpec((GMLP_CHUNK, w), lambda i: (i, 0))
    row_v = pl.BlockSpec((GMLP_CHUNK, w), lambda i: (i, 1))
    vec = pl.BlockSpec((1, w), lambda i: (0, 0))
    bs = jnp.repeat(b_s.T, LANES, axis=1)
    return pl.pallas_call(
        functools.partial(_sgu_body, groups=groups),
        out_shape=jax.ShapeDtypeStruct((t, w), BF16),
        grid=(t // GMLP_CHUNK,),
        in_specs=[row_u, row_v, vec, vec,
                  pl.BlockSpec((groups, GMLP_CHUNK, GMLP_CHUNK), lambda i: (0, 0, 0)),
                  pl.BlockSpec((GMLP_CHUNK, w), lambda i: (0, 0))],
        out_specs=row_u,
        compiler_params=_params("parallel"),
        name=name,
    )(z, z, norm_g.reshape(1, w), norm_b.reshape(1, w), w_s.astype(BF16), bs)


def kernel(x, mem, positions, ab_w_in, hgrn_lower_bounds, hgrn_norm_g, attn_sink, ab_w_out,
           gmlp_w_in, gmlp_norm_g, gmlp_norm_b, gmlp_w_spatial, gmlp_b_spatial, gmlp_w_out,
           xa_w_q, xa_w_k, xa_w_v, xa_w_o, ffn_w_gate, ffn_w_up, ffn_w_down, ln_g, ln_b):
    batch, seq, d = x.shape
    n_mem = mem.shape[1]
    depth = ln_g.shape[0]
    t = batch * seq
    alpha = (2.0 * depth) ** 0.25
    hgrn_w = d // 2
    attn_w = d - hgrn_w
    kv_w = attn_w // ATTN_GROUP

    lb_all = jnp.cumsum(jax.nn.softmax(hgrn_lower_bounds.astype(F32), axis=1), axis=1)

    h = x.reshape(t, d)
    h_bf = h.astype(BF16)
    mem_bf = mem.reshape(batch * n_mem, d).astype(BF16)

    for layer in range(depth):
        j = layer // 2
        tag = "l%d_" % layer
        if layer % 2 == 0:
            proj, w_mix, w_xq, w_xo = _matmul(
                [(h_bf, (ab_w_in, j), 0)], F32, tm=2048, tn=512, a_once=True,
                side_casts=[(ab_w_out, j), (xa_w_q, layer), (xa_w_o, layer)], name=tag + "mix_in")
            o_f, o_b = _hgrn_scan(proj, lb_all[:, layer], batch=batch, seq=seq, width=hgrn_w, tb=512,
                                  heads_per_step=2, name=tag + "hgrn_scan")
            a_out = _hgrn_out(o_f, o_b, proj, hgrn_norm_g[j], tr=256, name=tag + "hgrn_out")
            q_r, k_r, v_r = _rope(proj, positions, q_col=5 * hgrn_w, q_width=attn_w, kv_width=kv_w, tr=256,
                                  name=tag + "rope")
            b_out = _window_attention(q_r, k_r, v_r, attn_sink[j].astype(F32), batch=batch, seq=seq,
                                      kv_per_step=2, name=tag + "swa")
            mix_pairs = [(a_out, w_mix, 0), (b_out, w_mix, 1)]
        else:
            z, w_mix, w_xq, w_xo = _matmul(
                [(h_bf, (gmlp_w_in, j), 0)], F32, tm=2048, tn=512, epilogue=_gelu_exact, a_once=True,
                side_casts=[(gmlp_w_out, j), (xa_w_q, layer), (xa_w_o, layer)], name=tag + "gmlp_in")
            gated = _spatial_gating(z, gmlp_norm_g[j], gmlp_norm_b[j], gmlp_w_spatial[j], gmlp_b_spatial[j],
                                    name=tag + "gmlp_sgu")
            mix_pairs = [(gated, w_mix, 0)]
        h, h_bf = _matmul_ln(mix_pairs, h, ln_g[layer, 0], ln_b[layer, 0], alpha=alpha, tm=1024, tn=512,
                             name=tag + "mix_out_ln0")

        k_m = _matmul([(mem_bf, (xa_w_k, layer), 0)], BF16, tm=512, tn=512, name=tag + "xa_k")
        v_m = _matmul([(mem_bf, (xa_w_v, layer), 0)], BF16, tm=512, tn=512, name=tag + "xa_v")
        att = _cross_attention(h_bf, w_xq, k_m, v_m, batch=batch, seq=seq, mem=n_mem, tm=512, name=tag + "xa_attn")
        h, h_bf = _matmul_ln([(att, w_xo, 0)], h, ln_g[layer, 1], ln_b[layer, 1], alpha=alpha, tm=1024, tn=512,
                             name=tag + "xa_out_ln1")

        act, w_down = _ffn_up(h_bf, (ffn_w_gate, layer), (ffn_w_up, layer), tm=2048, tn=256,
                              side_casts=[(ffn_w_down, layer)], name=tag + "ffn_up")
        h, h_bf = _matmul_ln([(act, w_down, 0)], h, ln_g[layer, 2], ln_b[layer, 2], alpha=alpha, tm=512, tn=256,
                             name=tag + "ffn_down_ln2")
    return h.reshape(batch, seq, d)
```

```python
import functools
import math

import jax
import jax.numpy as jnp
from jax import lax
from jax.experimental import pallas as pl
from jax.experimental.pallas import tpu as pltpu

F32 = jnp.float32
BF16 = jnp.bfloat16

LANES = 128
SUBLANES = 8
BF16_ROWS = 16
HEAD_DIM = LANES
VMEM_LIMIT_BYTES = 58 * 1024 * 1024

HGRN_CHUNK = 128
HGRN_LEVELS = 7
LOG2_E = 1.4426950408889634
WINDOW = 128
ATTN_BLOCK = 128
ATTN_GROUP = 4
ROPE_THETA = 500000.0
ROPE_DIM = HEAD_DIM // 4
GMLP_CHUNK = 128
MEM_HEADS = 4
LN_EPS = 1e-5
RMS_EPS = 1e-6


def _params(*sem):
    return pltpu.CompilerParams(dimension_semantics=sem, vmem_limit_bytes=VMEM_LIMIT_BYTES)


def _pick(n, pref):
    t = min(pref, n)
    while n % t:
        t -= LANES
    return t


def _sigmoid(x):
    return 1.0 / (1.0 + jnp.exp(-x))


def _layer_norm_rows(y, g, b):
    mu = jnp.mean(y, axis=-1, keepdims=True)
    yc = y - mu
    var = jnp.mean(yc * yc, axis=-1, keepdims=True)
    return yc * lax.rsqrt(var + LN_EPS) * g + b


def _w_spec(w, k, tn, imap):
    if isinstance(w, tuple):
        stack, lead = w
        return pl.BlockSpec((None, k, tn), lambda *ids: (lead,) + tuple(imap(*ids))), stack
    return pl.BlockSpec((k, tn), imap), w


def _w_cols(w):
    return (w[0] if isinstance(w, tuple) else w).shape[-1]


def _identity(x):
    return x


def _gelu_exact(x):
    return 0.5 * x * (1.0 + lax.erf(x * (1.0 / math.sqrt(2.0))))


def _dot_pairs(refs, n_pairs):
    acc = None
    for p in range(n_pairs):
        d = jnp.dot(refs[2 * p][...], refs[2 * p + 1][...].astype(BF16), preferred_element_type=F32)
        acc = d if acc is None else acc + d
    return acc


def _side_cast_specs(casts, grid):
    n_steps = grid[0] * grid[1]
    in_specs, args, out_specs, out_shapes = [], [], [], []
    for stack, lead in casts:
        _, r, c = stack.shape
        rows = BF16_ROWS
        while r % rows or r // rows > n_steps:
            rows += BF16_ROWS
        last = r // rows - 1

        def slab(i, j, last=last):
            return jnp.minimum(i * grid[1] + j, last)
        in_specs.append(pl.BlockSpec((None, rows, c), lambda i, j, lead=lead, slab=slab: (lead, slab(i, j), 0)))
        out_specs.append(pl.BlockSpec((rows, c), lambda i, j, slab=slab: (slab(i, j), 0)))
        out_shapes.append(jax.ShapeDtypeStruct((r, c), BF16))
        args.append(stack)
    return in_specs, args, out_specs, out_shapes


def _run_side_casts(side_in, side_out):
    for x_ref, c_ref in zip(side_in, side_out):
        c_ref[...] = x_ref[...].astype(BF16)


def _mm_body(*refs, n_pairs, n_side, epilogue):
    n_in = 2 * n_pairs + n_side
    o_ref = refs[n_in]
    o_ref[...] = epilogue(_dot_pairs(refs, n_pairs)).astype(o_ref.dtype)
    _run_side_casts(refs[2 * n_pairs:n_in], refs[n_in + 1:])


def _pair_specs(pairs, tm, tn, a_once, row_tile=lambda i: i, col_tile=lambda i, j: j):
    in_specs, args = [], []
    a_mode = {"pipeline_mode": pl.Buffered(1)} if a_once else {}
    for a, w, kb in pairs:
        k = a.shape[1]
        w_spec, w_arg = _w_spec(w, k, tn, lambda i, j, kb=kb: (kb, col_tile(i, j)))
        in_specs += [pl.BlockSpec((tm, k), lambda i, j: (row_tile(i), 0), **a_mode), w_spec]
        args += [a, w_arg]
    return in_specs, args


def _matmul(pairs, out_dtype, *, tm, tn, epilogue=_identity, a_once=False, side_casts=(), name):
    m = pairs[0][0].shape[0]
    n = _w_cols(pairs[0][1])
    tm = _pick(m, tm)
    tn = _pick(n, tn)
    grid = (m // tm, n // tn)
    in_specs, args = _pair_specs(pairs, tm, tn, a_once)
    c_in, c_args, c_out, c_shapes = _side_cast_specs(side_casts, grid)
    outs = pl.pallas_call(
        functools.partial(_mm_body, n_pairs=len(pairs), n_side=len(c_args), epilogue=epilogue),
        out_shape=[jax.ShapeDtypeStruct((m, n), out_dtype)] + c_shapes,
        grid=grid,
        in_specs=in_specs + c_in,
        out_specs=[pl.BlockSpec((tm, tn), lambda i, j: (i, j))] + c_out,
        compiler_params=_params("arbitrary", "arbitrary"),
        name=name,
    )(*args, *c_args)
    return outs if side_casts else outs[0]


def _zero_ln_scratch(y_ref, mean_ref, m2_ref):
    y_ref[...] = jnp.zeros_like(y_ref)
    mean_ref[...] = jnp.zeros_like(mean_ref)
    m2_ref[...] = jnp.zeros_like(m2_ref)


def _normalise_staged(y_ref, mean_ref, m2_ref, g_ref, b_ref, of_ref, ob_ref, j, prev, inv_n):
    rstd = lax.rsqrt(m2_ref[prev] * inv_n + LN_EPS)
    o = (y_ref[j] - mean_ref[prev]) * rstd * g_ref[...] + b_ref[...]
    of_ref[...] = o
    ob_ref[...] = o.astype(BF16)


def _merge_row_stats(y, tile, on, tn, mean_ref, m2_ref, sel):
    m_j = jnp.mean(y, axis=-1, keepdims=True)
    d = y - m_j
    m2_j = jnp.sum(d * d, axis=-1, keepdims=True)
    one = jnp.ones((1, 1), F32)
    on_f = jnp.where(on, one, 0.0)
    n_a = (jnp.zeros((1, 1), jnp.int32) + tile).astype(F32) * tn
    n_b = on_f * tn
    w_b = n_b / jnp.maximum(n_a + n_b, 1.0)
    keep = jnp.where(n_a > 0.0, one, 1.0 - on_f)
    delta = m_j - mean_ref[sel]
    mean_ref[sel] = mean_ref[sel] + delta * w_b
    m2_ref[sel] = keep * m2_ref[sel] + on_f * m2_j + delta * delta * (n_a * w_b)


def _mm_ln_body(*refs, n_pairs, alpha, n_row, n_col, tn, inv_n):
    res_ref, g_ref, b_ref, of_ref, ob_ref, y_ref, mean_ref, m2_ref = refs[2 * n_pairs:]
    i = pl.program_id(0)
    j = pl.program_id(1)
    cur = i % 2
    norm_args = (y_ref, mean_ref, m2_ref, g_ref, b_ref, of_ref, ob_ref, j, 1 - cur, inv_n)

    @pl.when((i == 0) & (j == 0))
    def _():
        _zero_ln_scratch(y_ref, mean_ref, m2_ref)

    def merge_previous_tile():
        tile = lax.rem(j + n_col - 1, n_col)
        _merge_row_stats(y_ref[tile], tile, True, tn, mean_ref, m2_ref, jnp.where(j == 0, 1 - cur, cur))

    @pl.when(i < n_row)
    def _():
        merge_previous_tile()
        _normalise_staged(*norm_args)
        y_ref[j] = alpha * res_ref[...] + _dot_pairs(refs, n_pairs)

    @pl.when(i == n_row)
    def _():
        merge_previous_tile()
        _normalise_staged(*norm_args)


def _mm_ln_k2_body(a_ref, w_ref, res_ref, g_ref, b_ref, of_ref, ob_ref, y_ref, mean_ref, m2_ref, *,
                   alpha, n_row, n_col, tn, inv_n):
    i = pl.program_id(0)
    half = pl.program_id(1)
    j = pl.program_id(2)
    cur = i % 2
    last = n_col - 1
    norm_args = (y_ref, mean_ref, m2_ref, g_ref, b_ref, of_ref, ob_ref, j, 1 - cur, inv_n)

    @pl.when((i == 0) & (half == 0) & (j == 0))
    def _():
        _zero_ln_scratch(y_ref, mean_ref, m2_ref)

    def merge_last_tile_of_previous_row_tile():
        _merge_row_stats(y_ref[last], last, j == 0, tn, mean_ref, m2_ref, 1 - cur)

    @pl.when((i < n_row) & (half == 0))
    def _():
        merge_last_tile_of_previous_row_tile()
        _normalise_staged(*norm_args)
        y_ref[j] = alpha * res_ref[...] + jnp.dot(a_ref[...], w_ref[...], preferred_element_type=F32)

    @pl.when((i < n_row) & (half == 1))
    def _():
        tile = lax.rem(j + last, n_col)
        _merge_row_stats(y_ref[tile], tile, j > 0, tn, mean_ref, m2_ref, cur)
        y_ref[j] = y_ref[j] + jnp.dot(a_ref[...], w_ref[...], preferred_element_type=F32)

    @pl.when((i == n_row) & (half == 0))
    def _():
        merge_last_tile_of_previous_row_tile()
        _normalise_staged(*norm_args)


def _matmul_ln(pairs, res, g, b, *, alpha, tm, tn, name):
    m, n = res.shape
    tm = _pick(m, tm)
    tn = _pick(n, tn)
    n_row, n_col = m // tm, n // tn
    assert n_col >= 2, "the lagged row statistics need at least two column tiles"

    def row_tile(i):
        return jnp.minimum(i, n_row - 1)

    def col_tile(i, j):
        return jnp.where(i < n_row, j, n_col - 1)

    in_specs, args = _pair_specs(pairs, tm, tn, False, row_tile, col_tile)
    vec = pl.BlockSpec((1, tn), lambda i, j: (0, j))
    out = pl.BlockSpec((tm, tn), lambda i, j: (jnp.maximum(i - 1, 0), jnp.where(i == 0, 0, j)))
    return pl.pallas_call(
        functools.partial(_mm_ln_body, n_pairs=len(pairs), alpha=alpha, n_row=n_row, n_col=n_col, tn=tn,
                          inv_n=1.0 / n),
        out_shape=(jax.ShapeDtypeStruct((m, n), F32), jax.ShapeDtypeStruct((m, n), BF16)),
        grid=(n_row + 1, n_col),
        in_specs=in_specs + [pl.BlockSpec((tm, tn), lambda i, j: (row_tile(i), col_tile(i, j))), vec, vec],
        out_specs=(out, out),
        scratch_shapes=[pltpu.VMEM((n_col, tm, tn), F32), pltpu.VMEM((2, tm, 1), F32), pltpu.VMEM((2, tm, 1), F32)],
        compiler_params=_params("arbitrary", "arbitrary"),
        name=name,
    )(*args, res, g.reshape(1, n), b.reshape(1, n))


def _matmul_ln_k2(a, w, res, g, b, *, alpha, tm, tn, name):
    m, n = res.shape
    k = a.shape[1]
    tm = _pick(m, tm)
    tn = _pick(n, tn)
    n_row, n_col = m // tm, n // tn
    assert n_col >= 2, "the lagged row statistics need at least two column tiles"
    kh = k // 2
    last = n_col - 1

    def row_tile(i):
        return jnp.minimum(i, n_row - 1)

    def live(i, half, j):
        return jnp.where(i < n_row, j, last)

    def first_half(i, half, j):
        return jnp.where((half == 0) & (i < n_row), j, last)

    def norm_tile(i, half, j):
        return jnp.where(half == 0, j, last)

    out = pl.BlockSpec((tm, tn), lambda i, half, j: (jnp.maximum(i - 1, 0), jnp.where(i == 0, 0, norm_tile(i, half, j))))
    vec = pl.BlockSpec((1, tn), lambda i, half, j: (0, norm_tile(i, half, j)))
    return pl.pallas_call(
        functools.partial(_mm_ln_k2_body, alpha=alpha, n_row=n_row, n_col=n_col, tn=tn, inv_n=1.0 / n),
        out_shape=(jax.ShapeDtypeStruct((m, n), F32), jax.ShapeDtypeStruct((m, n), BF16)),
        grid=(n_row + 1, 2, n_col),
        in_specs=[pl.BlockSpec((tm, kh), lambda i, half, j: (row_tile(i), jnp.where(i < n_row, half, 1))),
                  pl.BlockSpec((kh, tn), lambda i, half, j: (jnp.where(i < n_row, half, 1), live(i, half, j))),
                  pl.BlockSpec((tm, tn), lambda i, half, j: (row_tile(i), first_half(i, half, j))),
                  vec, vec],
        out_specs=(out, out),
        scratch_shapes=[pltpu.VMEM((n_col, tm, tn), F32), pltpu.VMEM((2, tm, 1), F32), pltpu.VMEM((2, tm, 1), F32)],
        compiler_params=_params("arbitrary", "arbitrary", "arbitrary"),
        name=name,
    )(a, w, res, g.reshape(1, n), b.reshape(1, n))


def _ffn_up_body(a_ref, wg_ref, wu_ref, *refs, n_side):
    a = a_ref[...]
    g = jnp.dot(a, wg_ref[...].astype(BF16), preferred_element_type=F32)
    u = jnp.dot(a, wu_ref[...].astype(BF16), preferred_element_type=F32)
    o_ref = refs[n_side]
    o_ref[...] = (g * _sigmoid(g) * u).astype(o_ref.dtype)
    _run_side_casts(refs[:n_side], refs[n_side + 1:])


def _ffn_up(a, wg, wu, *, tm, tn, side_casts=(), name):
    m, k = a.shape
    n = _w_cols(wg)
    tm = _pick(m, tm)
    tn = _pick(n, tn)
    grid = (m // tm, n // tn)
    wg_spec, wg_arg = _w_spec(wg, k, tn, lambda i, j: (0, j))
    wu_spec, wu_arg = _w_spec(wu, k, tn, lambda i, j: (0, j))
    c_in, c_args, c_out, c_shapes = _side_cast_specs(side_casts, grid)
    outs = pl.pallas_call(
        functools.partial(_ffn_up_body, n_side=len(c_args)),
        out_shape=[jax.ShapeDtypeStruct((m, n), BF16)] + c_shapes,
        grid=grid,
        in_specs=[pl.BlockSpec((tm, k), lambda i, j: (i, 0), pipeline_mode=pl.Buffered(1)), wg_spec, wu_spec] + c_in,
        out_specs=[pl.BlockSpec((tm, tn), lambda i, j: (i, j))] + c_out,
        compiler_params=_params("arbitrary", "arbitrary"),
        name=name,
    )(a, wg_arg, wu_arg, *c_args)
    return outs if side_casts else outs[0]


def _hgrn_chunk(z, q_raw, v, lb, st, lvl, sub, backward):
    c_len = z.shape[0]
    groups = c_len // SUBLANES
    shape3 = (groups, SUBLANES, LANES)
    e = jnp.exp(-jnp.abs(z))
    log_sig = jnp.minimum(z, 0.0) - jnp.log(1.0 + e)
    sig_neg = jnp.where(z >= 0.0, e, 1.0) / (1.0 + e)
    a = jnp.log(lb)
    c = jnp.log1p(-lb) + log_sig
    g = (jnp.maximum(a, c) + jnp.log(1.0 + jnp.exp(-jnp.abs(a - c)))) * LOG2_E
    k = ((1.0 - lb) * sig_neg).reshape(shape3)
    q = (q_raw * _sigmoid(q_raw)).reshape(shape3)
    g = g.reshape(shape3)

    loc = g
    for j in range(3):
        sh = 1 << j
        loc = loc + jnp.where(sub >= sh, pltpu.roll(loc, sh, 1), 0.0)
    locx = loc - g
    tot = jnp.broadcast_to(loc[:, SUBLANES - 1:SUBLANES, :], shape3)
    base = [jnp.zeros((SUBLANES, LANES), F32)]
    for r in range(groups):
        base.append(base[r] + tot[r])
    b = loc + jnp.stack(base[:groups])
    bx = b - g

    def bounds(m):
        span = 1 << (m - 3)
        first = [(r // span) * span for r in range(groups)]
        return (jnp.stack([base[f] for f in first]), jnp.stack([base[f + span] for f in first]))

    def factors(q_exp, k_exp):
        q_m = (q * jnp.exp2(q_exp)).reshape(c_len, LANES).astype(BF16)
        k_m = (k * jnp.exp2(k_exp)).reshape(c_len, LANES).astype(BF16)
        return q_m, k_m

    scores = jnp.zeros((c_len, c_len), F32)
    end_l, start_l = loc, locx
    for m in range(HGRN_LEVELS):
        if m < 3:
            q_exp, k_exp = (end_l - locx, locx - start_l) if backward else (loc - start_l, end_l - loc)
            sh = 1 << m
            upper = (sub & sh) != 0
            end_l = jnp.where(upper, end_l, pltpu.roll(end_l, SUBLANES - sh, 1))
            start_l = jnp.where(upper, pltpu.roll(start_l, sh, 1), start_l)
        else:
            start, end = bounds(m)
            q_exp, k_exp = (end - bx, bx - start) if backward else (b - start, end - b)
        q_m, k_m = factors(q_exp, k_exp)
        p_m = lax.dot_general(q_m, k_m, (((1,), (1,)), ((), ())), preferred_element_type=F32)
        scores = jnp.where(lvl == m + 1, p_m, scores)
    total = base[groups]
    q_c, k_c = factors(total - bx, bx) if backward else factors(b, total - b)
    decay = jnp.exp2(total[0:1, :])

    q2 = q.reshape(c_len, LANES)
    k2 = k.reshape(c_len, LANES)
    v_bf = v.astype(BF16)
    diag = jnp.sum(q2 * k2, axis=1, keepdims=True)
    o = (jnp.dot(scores.astype(BF16), v_bf, preferred_element_type=F32)
         + diag * v
         + lax.dot_general(q_c, st.astype(BF16), (((1,), (1,)), ((), ())), preferred_element_type=F32))
    st_new = st * decay + lax.dot_general(v_bf, k_c, (((0,), (0,)), ((), ())), preferred_element_type=F32)
    return o, st_new


def _hgrn_body(qf_ref, zf_ref, vf_ref, qb_ref, zb_ref, vb_ref, lb_ref, of_ref, ob_ref, st_ref, *, n_chunks, heads):
    @pl.when(pl.program_id(2) == 0)
    def _():
        st_ref[...] = jnp.zeros_like(st_ref)

    c_len = HGRN_CHUNK
    row = lax.broadcasted_iota(jnp.int32, (c_len, c_len), 0)
    col = lax.broadcasted_iota(jnp.int32, (c_len, c_len), 1)
    lvl = 32 - lax.clz(row ^ col)
    lvl_f = jnp.where(row > col, lvl, 0)
    lvl_b = jnp.where(row < col, lvl, 0)
    sub = lax.broadcasted_iota(jnp.int32, (c_len // SUBLANES, SUBLANES, LANES), 1)

    def step(c, carry):
        rf = pl.ds(pl.multiple_of(c * c_len, c_len), c_len)
        rb = pl.ds(pl.multiple_of((n_chunks - 1 - c) * c_len, c_len), c_len)
        for h in range(heads):
            sl = slice(h * HEAD_DIM, (h + 1) * HEAD_DIM)
            o_f, st_f = _hgrn_chunk(zf_ref[rf, sl], qf_ref[rf, sl], vf_ref[rf, sl], lb_ref[0, :, sl],
                                    st_ref[0, h], lvl_f, sub, False)
            of_ref[rf, sl] = o_f
            st_ref[0, h] = st_f
            o_b, st_b = _hgrn_chunk(zb_ref[rb, sl], qb_ref[rb, sl], vb_ref[rb, sl], lb_ref[1, :, sl],
                                    st_ref[1, h], lvl_b, sub, True)
            ob_ref[rb, sl] = o_b
            st_ref[1, h] = st_b
        return carry

    lax.fori_loop(0, n_chunks, step, 0)


def _hgrn_scan(proj, lb, *, batch, seq, width, tb, heads_per_step, name):
    hw = heads_per_step * HEAD_DIM
    head_groups = width // hw
    tb = _pick(seq, tb)
    nb = seq // tb
    n_chunks = tb // HGRN_CHUNK

    def spec(col0, reverse):
        def imap(b, h, t):
            tt = (nb - 1 - t) if reverse else t
            return (b * nb + tt, col0 * head_groups + h)
        return pl.BlockSpec((tb, hw), imap)

    out_f = pl.BlockSpec((tb, hw), lambda b, h, t: (b * nb + t, h))
    out_b = pl.BlockSpec((tb, hw), lambda b, h, t: (b * nb + (nb - 1 - t), h))
    t_rows = batch * seq
    return pl.pallas_call(
        functools.partial(_hgrn_body, n_chunks=n_chunks, heads=heads_per_step),
        out_shape=(jax.ShapeDtypeStruct((t_rows, width), F32), jax.ShapeDtypeStruct((t_rows, width), F32)),
        grid=(batch, head_groups, nb),
        in_specs=[spec(0, False), spec(1, False), spec(3, False),
                  spec(0, True), spec(2, True), spec(3, True),
                  pl.BlockSpec((2, 1, hw), lambda b, h, t: (0, 0, h))],
        out_specs=(out_f, out_b),
        scratch_shapes=[pltpu.VMEM((2, heads_per_step, HEAD_DIM, HEAD_DIM), F32)],
        compiler_params=_params("parallel", "parallel", "arbitrary"),
        name=name,
    )(proj, proj, proj, proj, proj, proj, lb.reshape(2, 1, width))


def _hgrn_out_body(of_ref, ob_ref, g_ref, ng_ref, o_ref, *, heads):
    for h in range(heads):
        sl = slice(h * HEAD_DIM, (h + 1) * HEAD_DIM)
        o = of_ref[:, sl] + ob_ref[:, sl]
        o = o * lax.rsqrt(jnp.mean(o * o, axis=-1, keepdims=True) + RMS_EPS)
        gate = g_ref[:, sl]
        o_ref[:, sl] = (o * ng_ref[:, sl] * (gate * _sigmoid(gate))).astype(o_ref.dtype)


def _hgrn_out(o_f, o_b, proj, norm_g, *, tr, name):
    t, width = o_f.shape
    tr = _pick(t, tr)
    row = pl.BlockSpec((tr, width), lambda i: (i, 0))
    return pl.pallas_call(
        functools.partial(_hgrn_out_body, heads=width // HEAD_DIM),
        out_shape=jax.ShapeDtypeStruct((t, width), BF16),
        grid=(t // tr,),
        in_specs=[row, row, pl.BlockSpec((tr, width), lambda i: (i, 4)),
                  pl.BlockSpec((1, width), lambda i: (0, 0))],
        out_specs=row,
        compiler_params=_params("parallel"),
        name=name,
    )(o_f, o_b, proj, norm_g.reshape(1, width))


def _rope_body(pos_ref, invf_ref, q_ref, k_ref, v_ref, qo_ref, ko_ref, vo_ref, *, q_heads, kv_heads, scale):
    half = ROPE_DIM // 2
    ang = pos_ref[...].astype(F32) * invf_ref[...]
    lane = lax.broadcasted_iota(jnp.int32, ang.shape, 1)
    cos = jnp.where(lane < ROPE_DIM, jnp.cos(ang), 1.0)
    sin = jnp.sin(ang)
    sin = jnp.where(lane < half, -sin, jnp.where(lane < ROPE_DIM, sin, 0.0))

    def rot(x):
        partner = jnp.where(lane < half, pltpu.roll(x, LANES - half, 1), pltpu.roll(x, half, 1))
        return x * cos + partner * sin

    for h in range(q_heads):
        sl = slice(h * HEAD_DIM, (h + 1) * HEAD_DIM)
        qo_ref[:, sl] = (rot(q_ref[:, sl]) * scale).astype(BF16)
    for h in range(kv_heads):
        sl = slice(h * HEAD_DIM, (h + 1) * HEAD_DIM)
        ko_ref[:, sl] = rot(k_ref[:, sl]).astype(BF16)
    vo_ref[...] = v_ref[...].astype(BF16)


def _rope(proj, positions, *, q_col, q_width, kv_width, tr, name):
    t = proj.shape[0]
    tr = _pick(t, tr)
    half = ROPE_DIM // 2
    inv_freq = jnp.power(jnp.float32(ROPE_THETA), -jnp.arange(half, dtype=F32) * (2.0 / ROPE_DIM))
    invf = jnp.zeros((1, LANES), F32).at[0, :ROPE_DIM].set(jnp.concatenate([inv_freq, inv_freq]))
    k_blk = (q_col + q_width) // kv_width
    return pl.pallas_call(
        functools.partial(_rope_body, q_heads=q_width // HEAD_DIM, kv_heads=kv_width // HEAD_DIM,
                          scale=HEAD_DIM ** -0.5),
        out_shape=(jax.ShapeDtypeStruct((t, q_width), BF16), jax.ShapeDtypeStruct((t, kv_width), BF16),
                   jax.ShapeDtypeStruct((t, kv_width), BF16)),
        grid=(t // tr,),
        in_specs=[pl.BlockSpec((tr, 1), lambda i: (i, 0)),
                  pl.BlockSpec((1, LANES), lambda i: (0, 0)),
                  pl.BlockSpec((tr, q_width), lambda i: (i, q_col // q_width)),
                  pl.BlockSpec((tr, kv_width), lambda i: (i, k_blk)),
                  pl.BlockSpec((tr, kv_width), lambda i: (i, k_blk + 1))],
        out_specs=(pl.BlockSpec((tr, q_width), lambda i: (i, 0)),
                   pl.BlockSpec((tr, kv_width), lambda i: (i, 0)),
                   pl.BlockSpec((tr, kv_width), lambda i: (i, 0))),
        compiler_params=_params("parallel"),
        name=name,
    )(positions.reshape(t, 1), invf, proj, proj, proj)


def _swa_body(sink_ref, q_ref, kp_ref, kc_ref, kn_ref, vp_ref, vc_ref, vn_ref, o_ref, *, n_blocks, kv_per_step):
    n = pl.program_id(2)
    blk = ATTN_BLOCK
    shape = (ATTN_GROUP * blk, 3 * blk)
    i = lax.broadcasted_iota(jnp.int32, shape, 0) & (blk - 1)
    j = lax.broadcasted_iota(jnp.int32, shape, 1)
    rel = j - i
    valid = (rel >= blk - WINDOW) & (rel <= blk + WINDOW)
    valid &= (j >= blk) | (n > 0)
    valid &= (j < 2 * blk) | (n < n_blocks - 1)
    r = lax.broadcasted_iota(jnp.int32, (ATTN_GROUP * blk, 1), 0)
    for hh in range(kv_per_step):
        h = pl.program_id(1) * kv_per_step + hh
        kv = slice(hh * HEAD_DIM, (hh + 1) * HEAD_DIM)
        q0 = hh * ATTN_GROUP * HEAD_DIM
        q = jnp.concatenate([q_ref[:, q0 + g * HEAD_DIM:q0 + (g + 1) * HEAD_DIM] for g in range(ATTN_GROUP)], axis=0)
        k = jnp.concatenate([kp_ref[:, kv], kc_ref[:, kv], kn_ref[:, kv]], axis=0)
        v = jnp.concatenate([vp_ref[:, kv], vc_ref[:, kv], vn_ref[:, kv]], axis=0)
        s = lax.dot_general(q, k, (((1,), (1,)), ((), ())), preferred_element_type=F32)
        s = jnp.where(valid, s, -jnp.inf)
        sink = jnp.zeros((ATTN_GROUP * blk, 1), F32)
        for g in range(ATTN_GROUP):
            sink = jnp.where(r >= g * blk, sink_ref[h * ATTN_GROUP + g], sink)
        m = jnp.maximum(jnp.max(s, axis=-1, keepdims=True), sink)
        p = jnp.exp(s - m)
        denom = jnp.sum(p, axis=-1, keepdims=True) + jnp.exp(sink - m)
        o = jnp.dot(p.astype(BF16), v, preferred_element_type=F32) / denom
        for g in range(ATTN_GROUP):
            o_ref[:, q0 + g * HEAD_DIM:q0 + (g + 1) * HEAD_DIM] = o[g * blk:(g + 1) * blk, :].astype(o_ref.dtype)


def _window_attention(q, k, v, sink, *, batch, seq, kv_per_step, name):
    t, q_width = q.shape
    kv_heads = k.shape[1] // HEAD_DIM
    kv_per_step = math.gcd(kv_per_step, kv_heads)
    nb = seq // ATTN_BLOCK
    gw = kv_per_step * ATTN_GROUP * HEAD_DIM

    def kv_spec(shift):
        def imap(b, h, n):
            return (b * nb + jnp.clip(n + shift, 0, nb - 1), h)
        return pl.BlockSpec((ATTN_BLOCK, kv_per_step * HEAD_DIM), imap)

    q_spec = pl.BlockSpec((ATTN_BLOCK, gw), lambda b, h, n: (b * nb + n, h))
    return pl.pallas_call(
        functools.partial(_swa_body, n_blocks=nb, kv_per_step=kv_per_step),
        out_shape=jax.ShapeDtypeStruct((t, q_width), BF16),
        grid=(batch, kv_heads // kv_per_step, nb),
        in_specs=[pl.BlockSpec(memory_space=pltpu.SMEM), q_spec,
                  kv_spec(-1), kv_spec(0), kv_spec(1), kv_spec(-1), kv_spec(0), kv_spec(1)],
        out_specs=q_spec,
        compiler_params=_params("parallel", "parallel", "arbitrary"),
        name=name,
    )(sink, q, k, k, k, v, v, v)


def _xattn_body(a_ref, wq_ref, k_ref, v_ref, o_ref, *, scale):
    q = jnp.dot(a_ref[...], wq_ref[...], preferred_element_type=F32) * scale
    s = lax.dot_general(q.astype(BF16), k_ref[...], (((1,), (1,)), ((), ())), preferred_element_type=F32)
    m = jnp.max(s, axis=-1, keepdims=True)
    p = jnp.exp(s - m)
    denom = jnp.sum(p, axis=-1, keepdims=True)
    o = jnp.dot(p.astype(BF16), v_ref[...], preferred_element_type=F32) / denom
    o_ref[...] = o.astype(o_ref.dtype)


def _cross_attention(a, wq, k, v, *, batch, seq, mem, tm, name):
    t, d = a.shape
    hd = d // MEM_HEADS
    tm = _pick(seq, tm)
    nb = seq // tm
    return pl.pallas_call(
        functools.partial(_xattn_body, scale=hd ** -0.5),
        out_shape=jax.ShapeDtypeStruct((t, d), BF16),
        grid=(batch, nb, MEM_HEADS),
        in_specs=[pl.BlockSpec((tm, d), lambda b, i, h: (b * nb + i, 0)),
                  pl.BlockSpec((d, hd), lambda b, i, h: (0, h)),
                  pl.BlockSpec((mem, hd), lambda b, i, h: (b, h)),
                  pl.BlockSpec((mem, hd), lambda b, i, h: (b, h))],
        out_specs=pl.BlockSpec((tm, hd), lambda b, i, h: (b * nb + i, h)),
        compiler_params=_params("parallel", "parallel", "arbitrary"),
        name=name,
    )(a, wq, k, v)


def _sgu_body(u_ref, v_ref, g_ref, b_ref, ws_ref, bs_ref, o_ref, *, groups):
    vn = _layer_norm_rows(v_ref[...], g_ref[...], b_ref[...]).astype(BF16)
    for g in range(groups):
        sl = slice(g * LANES, (g + 1) * LANES)
        s = jnp.dot(ws_ref[g], vn[:, sl], preferred_element_type=F32) + bs_ref[:, sl]
        o_ref[:, sl] = (u_ref[:, sl] * s).astype(o_ref.dtype)


def _spatial_gating(z, norm_g, norm_b, w_s, b_s, *, name):
    t, w2 = z.shape
    w = w2 // 2
    groups = w // LANES
    row_u = pl.BlockSpec((GMLP_CHUNK, w), lambda i: (i, 0))
    row_v = pl.BlockSpec((GMLP_CHUNK, w), lambda i: (i, 1))
    vec = pl.BlockSpec((1, w), lambda i: (0, 0))
    bs = jnp.repeat(b_s.T, LANES, axis=1)
    return pl.pallas_call(
        functools.partial(_sgu_body, groups=groups),
        out_shape=jax.ShapeDtypeStruct((t, w), BF16),
        grid=(t // GMLP_CHUNK,),
        in_specs=[row_u, row_v, vec, vec,
                  pl.BlockSpec((groups, GMLP_CHUNK, GMLP_CHUNK), lambda i: (0, 0, 0)),
                  pl.BlockSpec((GMLP_CHUNK, w), lambda i: (0, 0))],
        out_specs=row_u,
        compiler_params=_params("parallel"),
        name=name,
    )(z, z, norm_g.reshape(1, w), norm_b.reshape(1, w), w_s.astype(BF16), bs)


def kernel(x, mem, positions, ab_w_in, hgrn_lower_bounds, hgrn_norm_g, attn_sink, ab_w_out,
           gmlp_w_in, gmlp_norm_g, gmlp_norm_b, gmlp_w_spatial, gmlp_b_spatial, gmlp_w_out,
           xa_w_q, xa_w_k, xa_w_v, xa_w_o, ffn_w_gate, ffn_w_up, ffn_w_down, ln_g, ln_b):
    batch, seq, d = x.shape
    n_mem = mem.shape[1]
    depth = ln_g.shape[0]
    t = batch * seq
    alpha = (2.0 * depth) ** 0.25
    hgrn_w = d // 2
    attn_w = d - hgrn_w
    kv_w = attn_w // ATTN_GROUP

    lb_all = jnp.cumsum(jax.nn.softmax(hgrn_lower_bounds.astype(F32), axis=1), axis=1)

    h = x.reshape(t, d)
    h_bf = h.astype(BF16)
    mem_bf = mem.reshape(batch * n_mem, d).astype(BF16)

    for layer in range(depth):
        j = layer // 2
        tag = "l%d_" % layer
        if layer % 2 == 0:
            proj, w_mix, w_xq, w_xo = _matmul(
                [(h_bf, (ab_w_in, j), 0)], F32, tm=2048, tn=512, a_once=True,
                side_casts=[(ab_w_out, j), (xa_w_q, layer), (xa_w_o, layer)], name=tag + "mix_in")
            o_f, o_b = _hgrn_scan(proj, lb_all[:, layer], batch=batch, seq=seq, width=hgrn_w, tb=512,
                                  heads_per_step=2, name=tag + "hgrn_scan")
            a_out = _hgrn_out(o_f, o_b, proj, hgrn_norm_g[j], tr=256, name=tag + "hgrn_out")
            q_r, k_r, v_r = _rope(proj, positions, q_col=5 * hgrn_w, q_width=attn_w, kv_width=kv_w, tr=256,
                                  name=tag + "rope")
            b_out = _window_attention(q_r, k_r, v_r, attn_sink[j].astype(F32), batch=batch, seq=seq,
                                      kv_per_step=2, name=tag + "swa")
            mix_pairs = [(a_out, w_mix, 0), (b_out, w_mix, 1)]
        else:
            z, w_mix, w_xq, w_xo = _matmul(
                [(h_bf, (gmlp_w_in, j), 0)], F32, tm=2048, tn=512, epilogue=_gelu_exact, a_once=True,
                side_casts=[(gmlp_w_out, j), (xa_w_q, layer), (xa_w_o, layer)], name=tag + "gmlp_in")
            gated = _spatial_gating(z, gmlp_norm_g[j], gmlp_norm_b[j], gmlp_w_spatial[j], gmlp_b_spatial[j],
                                    name=tag + "gmlp_sgu")
            mix_pairs = [(gated, w_mix, 0)]
        h, h_bf = _matmul_ln(mix_pairs, h, ln_g[layer, 0], ln_b[layer, 0], alpha=alpha, tm=1024, tn=512,
                             name=tag + "mix_out_ln0")

        k_m = _matmul([(mem_bf, (xa_w_k, layer), 0)], BF16, tm=512, tn=512, name=tag + "xa_k")
        v_m = _matmul([(mem_bf, (xa_w_v, layer), 0)], BF16, tm=512, tn=512, name=tag + "xa_v")
        att = _cross_attention(h_bf, w_xq, k_m, v_m, batch=batch, seq=seq, mem=n_mem, tm=512, name=tag + "xa_attn")
        h, h_bf = _matmul_ln([(att, w_xo, 0)], h, ln_g[layer, 1], ln_b[layer, 1], alpha=alpha, tm=1024, tn=512,
                             name=tag + "xa_out_ln1")

        act, w_down = _ffn_up(h_bf, (ffn_w_gate, layer), (ffn_w_up, layer), tm=2048, tn=256,
                              side_casts=[(ffn_w_down, layer)], name=tag + "ffn_up")
        h, h_bf = _matmul_ln_k2(act, w_down, h, ln_g[layer, 2], ln_b[layer, 2], alpha=alpha, tm=1024, tn=256,
                                name=tag + "ffn_down_ln2")
    return h.reshape(batch, seq, d)
```

```python
import functools
import math

import jax
import jax.numpy as jnp
from jax import lax
from jax.experimental import pallas as pl
from jax.experimental.pallas import tpu as pltpu

F32 = jnp.float32
BF16 = jnp.bfloat16

LANES = 128
SUBLANES = 8
BF16_ROWS = 16
HEAD_DIM = LANES
VMEM_LIMIT_BYTES = 58 * 1024 * 1024

HGRN_CHUNK = 128
HGRN_LEVELS = 7
LOG2_E = 1.4426950408889634
WINDOW = 128
ATTN_BLOCK = 128
ATTN_GROUP = 4
ROPE_THETA = 500000.0
ROPE_DIM = HEAD_DIM // 4
GMLP_CHUNK = 128
MEM_HEADS = 4
LN_EPS = 1e-5
RMS_EPS = 1e-6


def _params(*sem):
    return pltpu.CompilerParams(dimension_semantics=sem, vmem_limit_bytes=VMEM_LIMIT_BYTES)


def _pick(n, pref):
    t = min(pref, n)
    while n % t:
        t -= LANES
    return t


def _sigmoid(x):
    return 1.0 / (1.0 + jnp.exp(-x))


def _layer_norm_rows(y, g, b):
    mu = jnp.mean(y, axis=-1, keepdims=True)
    yc = y - mu
    var = jnp.mean(yc * yc, axis=-1, keepdims=True)
    return yc * lax.rsqrt(var + LN_EPS) * g + b


def _w_spec(w, k, tn, imap):
    if isinstance(w, tuple):
        stack, lead = w
        return pl.BlockSpec((None, k, tn), lambda *ids: (lead,) + tuple(imap(*ids))), stack
    return pl.BlockSpec((k, tn), imap), w


def _w_cols(w):
    return (w[0] if isinstance(w, tuple) else w).shape[-1]


def _identity(x):
    return x


def _gelu_exact(x):
    return 0.5 * x * (1.0 + lax.erf(x * (1.0 / math.sqrt(2.0))))


def _dot_pairs(refs, n_pairs):
    acc = None
    for p in range(n_pairs):
        d = jnp.dot(refs[2 * p][...], refs[2 * p + 1][...].astype(BF16), preferred_element_type=F32)
        acc = d if acc is None else acc + d
    return acc


def _side_cast_specs(casts, grid):
    n_steps = grid[0] * grid[1]
    in_specs, args, out_specs, out_shapes = [], [], [], []
    for stack, lead in casts:
        _, r, c = stack.shape
        rows = BF16_ROWS
        while r % rows or r // rows > n_steps:
            rows += BF16_ROWS
        last = r // rows - 1

        def slab(i, j, last=last):
            return jnp.minimum(i * grid[1] + j, last)
        in_specs.append(pl.BlockSpec((None, rows, c), lambda i, j, lead=lead, slab=slab: (lead, slab(i, j), 0)))
        out_specs.append(pl.BlockSpec((rows, c), lambda i, j, slab=slab: (slab(i, j), 0)))
        out_shapes.append(jax.ShapeDtypeStruct((r, c), BF16))
        args.append(stack)
    return in_specs, args, out_specs, out_shapes


def _run_side_casts(side_in, side_out):
    for x_ref, c_ref in zip(side_in, side_out):
        c_ref[...] = x_ref[...].astype(BF16)


def _mm_body(*refs, n_pairs, n_side, epilogue):
    n_in = 2 * n_pairs + n_side
    o_ref = refs[n_in]
    o_ref[...] = epilogue(_dot_pairs(refs, n_pairs)).astype(o_ref.dtype)
    _run_side_casts(refs[2 * n_pairs:n_in], refs[n_in + 1:])


def _pair_specs(pairs, tm, tn, a_once, row_tile=lambda i: i, col_tile=lambda i, j: j):
    in_specs, args = [], []
    a_mode = {"pipeline_mode": pl.Buffered(1)} if a_once else {}
    for a, w, kb in pairs:
        k = a.shape[1]
        w_spec, w_arg = _w_spec(w, k, tn, lambda i, j, kb=kb: (kb, col_tile(i, j)))
        in_specs += [pl.BlockSpec((tm, k), lambda i, j: (row_tile(i), 0), **a_mode), w_spec]
        args += [a, w_arg]
    return in_specs, args


def _matmul(pairs, out_dtype, *, tm, tn, epilogue=_identity, a_once=False, side_casts=(), name):
    m = pairs[0][0].shape[0]
    n = _w_cols(pairs[0][1])
    tm = _pick(m, tm)
    tn = _pick(n, tn)
    grid = (m // tm, n // tn)
    in_specs, args = _pair_specs(pairs, tm, tn, a_once)
    c_in, c_args, c_out, c_shapes = _side_cast_specs(side_casts, grid)
    outs = pl.pallas_call(
        functools.partial(_mm_body, n_pairs=len(pairs), n_side=len(c_args), epilogue=epilogue),
        out_shape=[jax.ShapeDtypeStruct((m, n), out_dtype)] + c_shapes,
        grid=grid,
        in_specs=in_specs + c_in,
        out_specs=[pl.BlockSpec((tm, tn), lambda i, j: (i, j))] + c_out,
        compiler_params=_params("arbitrary", "arbitrary"),
        name=name,
    )(*args, *c_args)
    return outs if side_casts else outs[0]


def _mm_ln_body(*refs, n_pairs, alpha, n_row, tn, inv_n):
    res_ref, g_ref, b_ref, of_ref, ob_ref, y_ref, mean_ref, m2_ref = refs[2 * n_pairs:]
    i = pl.program_id(0)
    j = pl.program_id(1)
    cur = i % 2
    prev = 1 - cur

    @pl.when((i == 0) & (j == 0))
    def _():
        y_ref[...] = jnp.zeros_like(y_ref)
        mean_ref[...] = jnp.zeros_like(mean_ref)
        m2_ref[...] = jnp.zeros_like(m2_ref)

    def normalise_previous():
        rstd = lax.rsqrt(m2_ref[prev] * inv_n + LN_EPS)
        o = (y_ref[j] - mean_ref[prev]) * rstd * g_ref[...] + b_ref[...]
        of_ref[...] = o
        ob_ref[...] = o.astype(BF16)

    @pl.when(i < n_row)
    def _():
        normalise_previous()
        y = alpha * res_ref[...] + _dot_pairs(refs, n_pairs)
        y_ref[j] = y
        m_j = jnp.mean(y, axis=-1, keepdims=True)
        d = y - m_j
        m2_j = jnp.sum(d * d, axis=-1, keepdims=True)
        n_a = (jnp.zeros((1, 1), jnp.int32) + j).astype(F32) * tn
        w_b = tn / (n_a + tn)
        keep = jnp.where(n_a > 0.0, 1.0, 0.0)
        delta = m_j - mean_ref[cur]
        mean_ref[cur] = mean_ref[cur] + delta * w_b
        m2_ref[cur] = keep * m2_ref[cur] + m2_j + delta * delta * (n_a * w_b)

    @pl.when(i == n_row)
    def _():
        normalise_previous()


def _matmul_ln(pairs, res, g, b, *, alpha, tm, tn, name):
    m, n = res.shape
    tm = _pick(m, tm)
    tn = _pick(n, tn)
    n_row, n_col = m // tm, n // tn

    def row_tile(i):
        return jnp.minimum(i, n_row - 1)

    def col_tile(i, j):
        return jnp.where(i < n_row, j, n_col - 1)

    in_specs, args = _pair_specs(pairs, tm, tn, False, row_tile, col_tile)
    vec = pl.BlockSpec((1, tn), lambda i, j: (0, j))
    out = pl.BlockSpec((tm, tn), lambda i, j: (jnp.maximum(i - 1, 0), jnp.where(i == 0, 0, j)))
    return pl.pallas_call(
        functools.partial(_mm_ln_body, n_pairs=len(pairs), alpha=alpha, n_row=n_row, tn=tn, inv_n=1.0 / n),
        out_shape=(jax.ShapeDtypeStruct((m, n), F32), jax.ShapeDtypeStruct((m, n), BF16)),
        grid=(n_row + 1, n_col),
        in_specs=in_specs + [pl.BlockSpec((tm, tn), lambda i, j: (row_tile(i), col_tile(i, j))), vec, vec],
        out_specs=(out, out),
        scratch_shapes=[pltpu.VMEM((n_col, tm, tn), F32), pltpu.VMEM((2, tm, 1), F32), pltpu.VMEM((2, tm, 1), F32)],
        compiler_params=_params("arbitrary", "arbitrary"),
        name=name,
    )(*args, res, g.reshape(1, n), b.reshape(1, n))


def _ffn_up_body(a_ref, wg_ref, wu_ref, *refs, n_side):
    a = a_ref[...]
    g = jnp.dot(a, wg_ref[...].astype(BF16), preferred_element_type=F32)
    u = jnp.dot(a, wu_ref[...].astype(BF16), preferred_element_type=F32)
    o_ref = refs[n_side]
    o_ref[...] = (g * _sigmoid(g) * u).astype(o_ref.dtype)
    _run_side_casts(refs[:n_side], refs[n_side + 1:])


def _ffn_up(a, wg, wu, *, tm, tn, side_casts=(), name):
    m, k = a.shape
    n = _w_cols(wg)
    tm = _pick(m, tm)
    tn = _pick(n, tn)
    grid = (m // tm, n // tn)
    wg_spec, wg_arg = _w_spec(wg, k, tn, lambda i, j: (0, j))
    wu_spec, wu_arg = _w_spec(wu, k, tn, lambda i, j: (0, j))
    c_in, c_args, c_out, c_shapes = _side_cast_specs(side_casts, grid)
    outs = pl.pallas_call(
        functools.partial(_ffn_up_body, n_side=len(c_args)),
        out_shape=[jax.ShapeDtypeStruct((m, n), BF16)] + c_shapes,
        grid=grid,
        in_specs=[pl.BlockSpec((tm, k), lambda i, j: (i, 0), pipeline_mode=pl.Buffered(1)), wg_spec, wu_spec] + c_in,
        out_specs=[pl.BlockSpec((tm, tn), lambda i, j: (i, j))] + c_out,
        compiler_params=_params("arbitrary", "arbitrary"),
        name=name,
    )(a, wg_arg, wu_arg, *c_args)
    return outs if side_casts else outs[0]


def _hgrn_chunk(z, q_raw, v, lb, st, lvl, sub, backward):
    c_len = z.shape[0]
    groups = c_len // SUBLANES
    shape3 = (groups, SUBLANES, LANES)
    e = jnp.exp(-jnp.abs(z))
    log_sig = jnp.minimum(z, 0.0) - jnp.log(1.0 + e)
    sig_neg = jnp.where(z >= 0.0, e, 1.0) / (1.0 + e)
    a = jnp.log(lb)
    c = jnp.log1p(-lb) + log_sig
    g = (jnp.maximum(a, c) + jnp.log(1.0 + jnp.exp(-jnp.abs(a - c)))) * LOG2_E
    k = ((1.0 - lb) * sig_neg).reshape(shape3)
    q = (q_raw * _sigmoid(q_raw)).reshape(shape3)
    g = g.reshape(shape3)

    loc = g
    for j in range(3):
        sh = 1 << j
        loc = loc + jnp.where(sub >= sh, pltpu.roll(loc, sh, 1), 0.0)
    locx = loc - g
    tot = jnp.broadcast_to(loc[:, SUBLANES - 1:SUBLANES, :], shape3)
    base = [jnp.zeros((SUBLANES, LANES), F32)]
    for r in range(groups):
        base.append(base[r] + tot[r])
    b = loc + jnp.stack(base[:groups])
    bx = b - g

    def bounds(m):
        span = 1 << (m - 3)
        first = [(r // span) * span for r in range(groups)]
        return (jnp.stack([base[f] for f in first]), jnp.stack([base[f + span] for f in first]))

    def factors(q_exp, k_exp):
        q_m = (q * jnp.exp2(q_exp)).reshape(c_len, LANES).astype(BF16)
        k_m = (k * jnp.exp2(k_exp)).reshape(c_len, LANES).astype(BF16)
        return q_m, k_m

    scores = jnp.zeros((c_len, c_len), F32)
    end_l, start_l = loc, locx
    for m in range(HGRN_LEVELS):
        if m < 3:
            q_exp, k_exp = (end_l - locx, locx - start_l) if backward else (loc - start_l, end_l - loc)
            sh = 1 << m
            upper = (sub & sh) != 0
            end_l = jnp.where(upper, end_l, pltpu.roll(end_l, SUBLANES - sh, 1))
            start_l = jnp.where(upper, pltpu.roll(start_l, sh, 1), start_l)
        else:
            start, end = bounds(m)
            q_exp, k_exp = (end - bx, bx - start) if backward else (b - start, end - b)
        q_m, k_m = factors(q_exp, k_exp)
        p_m = lax.dot_general(q_m, k_m, (((1,), (1,)), ((), ())), preferred_element_type=F32)
        scores = jnp.where(lvl == m + 1, p_m, scores)
    total = base[groups]
    q_c, k_c = factors(total - bx, bx) if backward else factors(b, total - b)
    decay = jnp.exp2(total[0:1, :])

    q2 = q.reshape(c_len, LANES)
    k2 = k.reshape(c_len, LANES)
    v_bf = v.astype(BF16)
    diag = jnp.sum(q2 * k2, axis=1, keepdims=True)
    o = (jnp.dot(scores.astype(BF16), v_bf, preferred_element_type=F32)
         + diag * v
         + lax.dot_general(q_c, st.astype(BF16), (((1,), (1,)), ((), ())), preferred_element_type=F32))
    st_new = st * decay + lax.dot_general(v_bf, k_c, (((0,), (0,)), ((), ())), preferred_element_type=F32)
    return o, st_new


def _hgrn_body(qf_ref, zf_ref, vf_ref, qb_ref, zb_ref, vb_ref, lb_ref, of_ref, ob_ref, st_ref, *, n_chunks, heads):
    @pl.when(pl.program_id(2) == 0)
    def _():
        st_ref[...] = jnp.zeros_like(st_ref)

    c_len = HGRN_CHUNK
    row = lax.broadcasted_iota(jnp.int32, (c_len, c_len), 0)
    col = lax.broadcasted_iota(jnp.int32, (c_len, c_len), 1)
    lvl = 32 - lax.clz(row ^ col)
    lvl_f = jnp.where(row > col, lvl, 0)
    lvl_b = jnp.where(row < col, lvl, 0)
    sub = lax.broadcasted_iota(jnp.int32, (c_len // SUBLANES, SUBLANES, LANES), 1)

    def step(c, carry):
        rf = pl.ds(pl.multiple_of(c * c_len, c_len), c_len)
        rb = pl.ds(pl.multiple_of((n_chunks - 1 - c) * c_len, c_len), c_len)
        for h in range(heads):
            sl = slice(h * HEAD_DIM, (h + 1) * HEAD_DIM)
            o_f, st_f = _hgrn_chunk(zf_ref[rf, sl], qf_ref[rf, sl], vf_ref[rf, sl], lb_ref[0, :, sl],
                                    st_ref[0, h], lvl_f, sub, False)
            of_ref[rf, sl] = o_f
            st_ref[0, h] = st_f
            o_b, st_b = _hgrn_chunk(zb_ref[rb, sl], qb_ref[rb, sl], vb_ref[rb, sl], lb_ref[1, :, sl],
                                    st_ref[1, h], lvl_b, sub, True)
            ob_ref[rb, sl] = o_b
            st_ref[1, h] = st_b
        return carry

    lax.fori_loop(0, n_chunks, step, 0)


def _hgrn_scan(proj, lb, *, batch, seq, width, tb, heads_per_step, name):
    hw = heads_per_step * HEAD_DIM
    head_groups = width // hw
    tb = _pick(seq, tb)
    nb = seq // tb
    n_chunks = tb // HGRN_CHUNK

    def spec(col0, reverse):
        def imap(b, h, t):
            tt = (nb - 1 - t) if reverse else t
            return (b * nb + tt, col0 * head_groups + h)
        return pl.BlockSpec((tb, hw), imap)

    out_f = pl.BlockSpec((tb, hw), lambda b, h, t: (b * nb + t, h))
    out_b = pl.BlockSpec((tb, hw), lambda b, h, t: (b * nb + (nb - 1 - t), h))
    t_rows = batch * seq
    return pl.pallas_call(
        functools.partial(_hgrn_body, n_chunks=n_chunks, heads=heads_per_step),
        out_shape=(jax.ShapeDtypeStruct((t_rows, width), F32), jax.ShapeDtypeStruct((t_rows, width), F32)),
        grid=(batch, head_groups, nb),
        in_specs=[spec(0, False), spec(1, False), spec(3, False),
                  spec(0, True), spec(2, True), spec(3, True),
                  pl.BlockSpec((2, 1, hw), lambda b, h, t: (0, 0, h))],
        out_specs=(out_f, out_b),
        scratch_shapes=[pltpu.VMEM((2, heads_per_step, HEAD_DIM, HEAD_DIM), F32)],
        compiler_params=_params("parallel", "parallel", "arbitrary"),
        name=name,
    )(proj, proj, proj, proj, proj, proj, lb.reshape(2, 1, width))


def _hgrn_out_body(of_ref, ob_ref, g_ref, ng_ref, o_ref, *, heads):
    for h in range(heads):
        sl = slice(h * HEAD_DIM, (h + 1) * HEAD_DIM)
        o = of_ref[:, sl] + ob_ref[:, sl]
        o = o * lax.rsqrt(jnp.mean(o * o, axis=-1, keepdims=True) + RMS_EPS)
        gate = g_ref[:, sl]
        o_ref[:, sl] = (o * ng_ref[:, sl] * (gate * _sigmoid(gate))).astype(o_ref.dtype)


def _hgrn_out(o_f, o_b, proj, norm_g, *, tr, name):
    t, width = o_f.shape
    tr = _pick(t, tr)
    row = pl.BlockSpec((tr, width), lambda i: (i, 0))
    return pl.pallas_call(
        functools.partial(_hgrn_out_body, heads=width // HEAD_DIM),
        out_shape=jax.ShapeDtypeStruct((t, width), BF16),
        grid=(t // tr,),
        in_specs=[row, row, pl.BlockSpec((tr, width), lambda i: (i, 4)),
                  pl.BlockSpec((1, width), lambda i: (0, 0))],
        out_specs=row,
        compiler_params=_params("parallel"),
        name=name,
    )(o_f, o_b, proj, norm_g.reshape(1, width))


def _rope_body(pos_ref, invf_ref, q_ref, k_ref, v_ref, qo_ref, ko_ref, vo_ref, *, q_heads, kv_heads, scale):
    half = ROPE_DIM // 2
    ang = pos_ref[...].astype(F32) * invf_ref[...]
    lane = lax.broadcasted_iota(jnp.int32, ang.shape, 1)
    cos = jnp.where(lane < ROPE_DIM, jnp.cos(ang), 1.0)
    sin = jnp.sin(ang)
    sin = jnp.where(lane < half, -sin, jnp.where(lane < ROPE_DIM, sin, 0.0))

    def rot(x):
        partner = jnp.where(lane < half, pltpu.roll(x, LANES - half, 1), pltpu.roll(x, half, 1))
        return x * cos + partner * sin

    for h in range(q_heads):
        sl = slice(h * HEAD_DIM, (h + 1) * HEAD_DIM)
        qo_ref[:, sl] = (rot(q_ref[:, sl]) * scale).astype(BF16)
    for h in range(kv_heads):
        sl = slice(h * HEAD_DIM, (h + 1) * HEAD_DIM)
        ko_ref[:, sl] = rot(k_ref[:, sl]).astype(BF16)
    vo_ref[...] = v_ref[...].astype(BF16)


def _rope(proj, positions, *, q_col, q_width, kv_width, tr, name):
    t = proj.shape[0]
    tr = _pick(t, tr)
    half = ROPE_DIM // 2
    inv_freq = jnp.power(jnp.float32(ROPE_THETA), -jnp.arange(half, dtype=F32) * (2.0 / ROPE_DIM))
    invf = jnp.zeros((1, LANES), F32).at[0, :ROPE_DIM].set(jnp.concatenate([inv_freq, inv_freq]))
    k_blk = (q_col + q_width) // kv_width
    return pl.pallas_call(
        functools.partial(_rope_body, q_heads=q_width // HEAD_DIM, kv_heads=kv_width // HEAD_DIM,
                          scale=HEAD_DIM ** -0.5),
        out_shape=(jax.ShapeDtypeStruct((t, q_width), BF16), jax.ShapeDtypeStruct((t, kv_width), BF16),
                   jax.ShapeDtypeStruct((t, kv_width), BF16)),
        grid=(t // tr,),
        in_specs=[pl.BlockSpec((tr, 1), lambda i: (i, 0)),
                  pl.BlockSpec((1, LANES), lambda i: (0, 0)),
                  pl.BlockSpec((tr, q_width), lambda i: (i, q_col // q_width)),
                  pl.BlockSpec((tr, kv_width), lambda i: (i, k_blk)),
                  pl.BlockSpec((tr, kv_width), lambda i: (i, k_blk + 1))],
        out_specs=(pl.BlockSpec((tr, q_width), lambda i: (i, 0)),
                   pl.BlockSpec((tr, kv_width), lambda i: (i, 0)),
                   pl.BlockSpec((tr, kv_width), lambda i: (i, 0))),
        compiler_params=_params("parallel"),
        name=name,
    )(positions.reshape(t, 1), invf, proj, proj, proj)


def _swa_body(sink_ref, q_ref, kp_ref, kc_ref, kn_ref, vp_ref, vc_ref, vn_ref, o_ref, *, n_blocks, kv_per_step):
    n = pl.program_id(2)
    blk = ATTN_BLOCK
    shape = (ATTN_GROUP * blk, 3 * blk)
    i = lax.broadcasted_iota(jnp.int32, shape, 0) & (blk - 1)
    j = lax.broadcasted_iota(jnp.int32, shape, 1)
    rel = j - i
    valid = (rel >= blk - WINDOW) & (rel <= blk + WINDOW)
    valid &= (j >= blk) | (n > 0)
    valid &= (j < 2 * blk) | (n < n_blocks - 1)
    r = lax.broadcasted_iota(jnp.int32, (ATTN_GROUP * blk, 1), 0)
    for hh in range(kv_per_step):
        h = pl.program_id(1) * kv_per_step + hh
        kv = slice(hh * HEAD_DIM, (hh + 1) * HEAD_DIM)
        q0 = hh * ATTN_GROUP * HEAD_DIM
        q = jnp.concatenate([q_ref[:, q0 + g * HEAD_DIM:q0 + (g + 1) * HEAD_DIM] for g in range(ATTN_GROUP)], axis=0)
        k = jnp.concatenate([kp_ref[:, kv], kc_ref[:, kv], kn_ref[:, kv]], axis=0)
        v = jnp.concatenate([vp_ref[:, kv], vc_ref[:, kv], vn_ref[:, kv]], axis=0)
        s = lax.dot_general(q, k, (((1,), (1,)), ((), ())), preferred_element_type=F32)
        s = jnp.where(valid, s, -jnp.inf)
        sink = jnp.zeros((ATTN_GROUP * blk, 1), F32)
        for g in range(ATTN_GROUP):
            sink = jnp.where(r >= g * blk, sink_ref[h * ATTN_GROUP + g], sink)
        m = jnp.maximum(jnp.max(s, axis=-1, keepdims=True), sink)
        p = jnp.exp(s - m)
        denom = jnp.sum(p, axis=-1, keepdims=True) + jnp.exp(sink - m)
        o = jnp.dot(p.astype(BF16), v, preferred_element_type=F32) / denom
        for g in range(ATTN_GROUP):
            o_ref[:, q0 + g * HEAD_DIM:q0 + (g + 1) * HEAD_DIM] = o[g * blk:(g + 1) * blk, :].astype(o_ref.dtype)


def _window_attention(q, k, v, sink, *, batch, seq, kv_per_step, name):
    t, q_width = q.shape
    kv_heads = k.shape[1] // HEAD_DIM
    kv_per_step = math.gcd(kv_per_step, kv_heads)
    nb = seq // ATTN_BLOCK
    gw = kv_per_step * ATTN_GROUP * HEAD_DIM

    def kv_spec(shift):
        def imap(b, h, n):
            return (b * nb + jnp.clip(n + shift, 0, nb - 1), h)
        return pl.BlockSpec((ATTN_BLOCK, kv_per_step * HEAD_DIM), imap)

    q_spec = pl.BlockSpec((ATTN_BLOCK, gw), lambda b, h, n: (b * nb + n, h))
    return pl.pallas_call(
        functools.partial(_swa_body, n_blocks=nb, kv_per_step=kv_per_step),
        out_shape=jax.ShapeDtypeStruct((t, q_width), BF16),
        grid=(batch, kv_heads // kv_per_step, nb),
        in_specs=[pl.BlockSpec(memory_space=pltpu.SMEM), q_spec,
                  kv_spec(-1), kv_spec(0), kv_spec(1), kv_spec(-1), kv_spec(0), kv_spec(1)],
        out_specs=q_spec,
        compiler_params=_params("parallel", "parallel", "arbitrary"),
        name=name,
    )(sink, q, k, k, k, v, v, v)


def _xattn_body(a_ref, wq_ref, k_ref, v_ref, o_ref, *, scale):
    q = jnp.dot(a_ref[...], wq_ref[...], preferred_element_type=F32) * scale
    s = lax.dot_general(q.astype(BF16), k_ref[...], (((1,), (1,)), ((), ())), preferred_element_type=F32)
    m = jnp.max(s, axis=-1, keepdims=True)
    p = jnp.exp(s - m)
    denom = jnp.sum(p, axis=-1, keepdims=True)
    o = jnp.dot(p.astype(BF16), v_ref[...], preferred_element_type=F32) / denom
    o_ref[...] = o.astype(o_ref.dtype)


def _cross_attention(a, wq, k, v, *, batch, seq, mem, tm, name):
    t, d = a.shape
    hd = d // MEM_HEADS
    tm = _pick(seq, tm)
    nb = seq // tm
    return pl.pallas_call(
        functools.partial(_xattn_body, scale=hd ** -0.5),
        out_shape=jax.ShapeDtypeStruct((t, d), BF16),
        grid=(batch, nb, MEM_HEADS),
        in_specs=[pl.BlockSpec((tm, d), lambda b, i, h: (b * nb + i, 0)),
                  pl.BlockSpec((d, hd), lambda b, i, h: (0, h)),
                  pl.BlockSpec((mem, hd), lambda b, i, h: (b, h)),
                  pl.BlockSpec((mem, hd), lambda b, i, h: (b, h))],
        out_specs=pl.BlockSpec((tm, hd), lambda b, i, h: (b * nb + i, h)),
        compiler_params=_params("parallel", "parallel", "arbitrary"),
        name=name,
    )(a, wq, k, v)


def _sgu_body(u_ref, v_ref, g_ref, b_ref, ws_ref, bs_ref, o_ref, *, groups):
    vn = _layer_norm_rows(v_ref[...].astype(F32), g_ref[...], b_ref[...]).astype(BF16)
    for g in range(groups):
        sl = slice(g * LANES, (g + 1) * LANES)
        s = jnp.dot(ws_ref[g], vn[:, sl], preferred_element_type=F32) + bs_ref[:, sl]
        o_ref[:, sl] = (u_ref[:, sl].astype(F32) * s).astype(o_ref.dtype)


def _spatial_gating(z, norm_g, norm_b, w_s, b_s, *, name):
    t, w2 = z.shape
    w = w2 // 2
    groups = w // LANES
    row_u = pl.BlockSpec((GMLP_CHUNK, w), lambda i: (i, 0))
    row_v = pl.BlockSpec((GMLP_CHUNK, w), lambda i: (i, 1))
    vec = pl.BlockSpec((1, w), lambda i: (0, 0))
    bs = jnp.repeat(b_s.T, LANES, axis=1)
    return pl.pallas_call(
        functools.partial(_sgu_body, groups=groups),
        out_shape=jax.ShapeDtypeStruct((t, w), BF16),
        grid=(t // GMLP_CHUNK,),
        in_specs=[row_u, row_v, vec, vec,
                  pl.BlockSpec((groups, GMLP_CHUNK, GMLP_CHUNK), lambda i: (0, 0, 0)),
                  pl.BlockSpec((GMLP_CHUNK, w), lambda i: (0, 0))],
        out_specs=row_u,
        compiler_params=_params("parallel"),
        name=name,
    )(z, z, norm_g.reshape(1, w), norm_b.reshape(1, w), w_s.astype(BF16), bs)


def kernel(x, mem, positions, ab_w_in, hgrn_lower_bounds, hgrn_norm_g, attn_sink, ab_w_out,
           gmlp_w_in, gmlp_norm_g, gmlp_norm_b, gmlp_w_spatial, gmlp_b_spatial, gmlp_w_out,
           xa_w_q, xa_w_k, xa_w_v, xa_w_o, ffn_w_gate, ffn_w_up, ffn_w_down, ln_g, ln_b):
    batch, seq, d = x.shape
    n_mem = mem.shape[1]
    depth = ln_g.shape[0]
    t = batch * seq
    alpha = (2.0 * depth) ** 0.25
    hgrn_w = d // 2
    attn_w = d - hgrn_w
    kv_w = attn_w // ATTN_GROUP

    lb_all = jnp.cumsum(jax.nn.softmax(hgrn_lower_bounds.astype(F32), axis=1), axis=1)

    h = x.reshape(t, d)
    h_bf = h.astype(BF16)
    mem_bf = mem.reshape(batch * n_mem, d).astype(BF16)

    for layer in range(depth):
        j = layer // 2
        tag = "l%d_" % layer
        if layer % 2 == 0:
            proj, w_mix, w_xq, w_xo = _matmul(
                [(h_bf, (ab_w_in, j), 0)], F32, tm=2048, tn=512, a_once=True,
                side_casts=[(ab_w_out, j), (xa_w_q, layer), (xa_w_o, layer)], name=tag + "mix_in")
            o_f, o_b = _hgrn_scan(proj, lb_all[:, layer], batch=batch, seq=seq, width=hgrn_w, tb=512,
                                  heads_per_step=4, name=tag + "hgrn_scan")
            a_out = _hgrn_out(o_f, o_b, proj, hgrn_norm_g[j], tr=256, name=tag + "hgrn_out")
            q_r, k_r, v_r = _rope(proj, positions, q_col=5 * hgrn_w, q_width=attn_w, kv_width=kv_w, tr=256,
                                  name=tag + "rope")
            b_out = _window_attention(q_r, k_r, v_r, attn_sink[j].astype(F32), batch=batch, seq=seq,
                                      kv_per_step=2, name=tag + "swa")
            mix_pairs = [(a_out, w_mix, 0), (b_out, w_mix, 1)]
        else:
            z, w_mix, w_xq, w_xo = _matmul(
                [(h_bf, (gmlp_w_in, j), 0)], BF16, tm=2048, tn=512, epilogue=_gelu_exact, a_once=True,
                side_casts=[(gmlp_w_out, j), (xa_w_q, layer), (xa_w_o, layer)], name=tag + "gmlp_in")
            gated = _spatial_gating(z, gmlp_norm_g[j], gmlp_norm_b[j], gmlp_w_spatial[j], gmlp_b_spatial[j],
                                    name=tag + "gmlp_sgu")
            mix_pairs = [(gated, w_mix, 0)]
        h, h_bf = _matmul_ln(mix_pairs, h, ln_g[layer, 0], ln_b[layer, 0], alpha=alpha, tm=1024, tn=512,
                             name=tag + "mix_out_ln0")

        k_m = _matmul([(mem_bf, (xa_w_k, layer), 0)], BF16, tm=512, tn=512, name=tag + "xa_k")
        v_m = _matmul([(mem_bf, (xa_w_v, layer), 0)], BF16, tm=512, tn=512, name=tag + "xa_v")
        att = _cross_attention(h_bf, w_xq, k_m, v_m, batch=batch, seq=seq, mem=n_mem, tm=512, name=tag + "xa_attn")
        h, h_bf = _matmul_ln([(att, w_xo, 0)], h, ln_g[layer, 1], ln_b[layer, 1], alpha=alpha, tm=1024, tn=512,
                             name=tag + "xa_out_ln1")

        act, w_down = _ffn_up(h_bf, (ffn_w_gate, layer), (ffn_w_up, layer), tm=2048, tn=256,
                              side_casts=[(ffn_w_down, layer)], name=tag + "ffn_up")
        h, h_bf = _matmul_ln([(act, w_down, 0)], h, ln_g[layer, 2], ln_b[layer, 2], alpha=alpha, tm=512, tn=256,
                             name=tag + "ffn_down_ln2")
    return h.reshape(batch, seq, d)
```
